```python
import math
import jax, jax.numpy as jnp
from jax import lax
import numpy as np

D_MODEL = 1024
BATCH = 2
SEQ = 8192
DEPTH = 4
DEC_BATCH = 32
DEC_SEQ = 4
PAST_LEN = 8192
PAGE_SIZE = 128

D_MIX = D_MODEL
DA_WIDTH = D_MIX // 4
SSM_WIDTH = D_MIX // 2
FOX_WIDTH = D_MIX - DA_WIDTH - SSM_WIDTH
DA_VDIM = 64
DA_HEADS = DA_WIDTH // DA_VDIM
DA_QK = DA_VDIM // 2
DA_KDIM = 2 * DA_QK
SSM_HEAD_DIM = 64
SSM_HEADS = SSM_WIDTH // SSM_HEAD_DIM
SSM_GROUPS = 2
SSM_STATE = 128
CONV_W = 4
CONV_CH = SSM_WIDTH + 2 * SSM_GROUPS * SSM_STATE
SSD_CHUNK = 128
FOX_HEAD_DIM = 64
FOX_HEADS = FOX_WIDTH // FOX_HEAD_DIM
N_BUCKETS = 32
MAX_DISTANCE = 128
Q_BLOCK = 128
EPS = 1e-6
NEG_INF = -1e30
IN_SIZES = (DA_HEADS * DA_KDIM, DA_HEADS * DA_KDIM, DA_WIDTH, DA_WIDTH,
            SSM_WIDTH, CONV_CH, SSM_HEADS,
            FOX_WIDTH, FOX_WIDTH, FOX_WIDTH, FOX_WIDTH, FOX_HEADS)
D_IN = sum(IN_SIZES)

kernel_name = "hybrid_diffattn_ssd_fox_step"


def rmsnorm(x, g):
    xf = x.astype(jnp.float32)
    y = xf * lax.rsqrt(jnp.mean(xf * xf, axis=-1, keepdims=True) + EPS)
    return (y * g.astype(jnp.float32)).astype(x.dtype)


def t5_bucket(rel):
    n = jnp.maximum(rel, 0)
    max_exact = N_BUCKETS // 2
    nf = jnp.maximum(n, 1).astype(jnp.float32)
    large = max_exact + (jnp.log(nf / max_exact) / math.log(MAX_DISTANCE / max_exact)
                         * (N_BUCKETS - max_exact)).astype(jnp.int32)
    large = jnp.minimum(large, N_BUCKETS - 1)
    return jnp.where(n < max_exact, n, large)


def sweep_query_blocks(fn, q_args):
    b, lq = q_args[0].shape[:2]
    blk = min(Q_BLOCK, lq)
    nb = -(-lq // blk)
    pad = nb * blk - lq

    def prep(a):
        if pad:
            a = jnp.pad(a, [(0, 0), (0, pad)] + [(0, 0)] * (a.ndim - 2))
        a = a.reshape((b, nb, blk) + a.shape[2:])
        return jnp.moveaxis(a, 1, 0)

    out = lax.map(fn, tuple(prep(a) for a in q_args))
    out = jnp.moveaxis(out, 0, 1).reshape((b, nb * blk) + out.shape[3:])
    return out[:, :lq]


def diff_attention(q, k, v, qpos, kpos, lam, rel_bias):
    scale = DA_QK ** -0.5
    table = rel_bias.astype(jnp.float32)

    def block(args):
        qb, qp = args
        s = jnp.einsum("bqhcd,bkhcd->bhcqk", qb, k, preferred_element_type=jnp.float32) * scale
        rel = qp[0][:, None] - kpos[None, :]
        bias = jnp.moveaxis(table[t5_bucket(rel)], -1, 0)
        s = jnp.where(rel >= 0, s + bias[None, :, None], NEG_INF)
        p = jax.nn.softmax(s, axis=-1)
        a = p[:, :, 0] - lam * p[:, :, 1]
        return jnp.einsum("bhqk,bkhd->bqhd", a.astype(v.dtype), v)

    return sweep_query_blocks(block, (q, qpos))


def forgetting_attention(q, k, v, cq, ck, qpos, kpos):
    scale = FOX_HEAD_DIM ** -0.5
    ck_t = jnp.swapaxes(ck, 1, 2)

    def block(args):
        qb, cqb, qp = args
        s = jnp.einsum("bqhd,bkhd->bhqk", qb, k, preferred_element_type=jnp.float32) * scale
        s = s + jnp.swapaxes(cqb, 1, 2)[..., None] - ck_t[:, :, None, :]
        s = jnp.where(qp[0][:, None] >= kpos[None, :], s, NEG_INF)
        p = jax.nn.softmax(s, axis=-1)
        return jnp.einsum("bhqk,bkhd->bqhd", p.astype(v.dtype), v)

    return sweep_query_blocks(block, (q, cq, qpos))


def ssd_scan(x, dA, Bm, Cm, init):
    b, L, h, p = x.shape
    n = Bm.shape[-1]
    cl = min(SSD_CHUNK, L)
    nc = -(-L // cl)
    pad = nc * cl - L
    if pad:
        x = jnp.pad(x, ((0, 0), (0, pad), (0, 0), (0, 0)))
        dA = jnp.pad(dA, ((0, 0), (0, pad), (0, 0)))
        Bm = jnp.pad(Bm, ((0, 0), (0, pad), (0, 0), (0, 0)))
        Cm = jnp.pad(Cm, ((0, 0), (0, pad), (0, 0), (0, 0)))
    x = x.reshape(b, nc, cl, h, p)
    Bm = Bm.reshape(b, nc, cl, h, n)
    Cm = Cm.reshape(b, nc, cl, h, n)
    a_cs = jnp.cumsum(dA.reshape(b, nc, cl, h), axis=2)
    causal = jnp.tril(jnp.ones((cl, cl), dtype=bool))[None, None, :, :, None]
    seg = a_cs[:, :, :, None, :] - a_cs[:, :, None, :, :]
    decay = jnp.exp(jnp.where(causal, seg, -jnp.inf))
    scores = jnp.einsum("bclhn,bcshn->bclsh", Cm, Bm) * decay
    y_diag = jnp.einsum("bclsh,bcshp->bclhp", scores, x)
    decay_to_end = jnp.exp(a_cs[:, :, -1:, :] - a_cs)
    chunk_states = jnp.einsum("bclhn,bclh,bclhp->bchpn", Bm, decay_to_end, x)
    chunk_decay = jnp.exp(a_cs[:, :, -1, :])

    def step(s, inp):
        cs, cd = inp
        return s * cd[:, :, None, None] + cs, s

    final, prev = lax.scan(step, init.astype(jnp.float32),
                           (jnp.moveaxis(chunk_states, 1, 0), jnp.moveaxis(chunk_decay, 1, 0)))
    prev = jnp.moveaxis(prev, 0, 1)
    y_off = jnp.einsum("bclhn,bchpn,bclh->bclhp", Cm, prev, jnp.exp(a_cs))
    y = (y_diag + y_off).reshape(b, nc * cl, h, p)[:, :L]
    return y, final


def mixer_layer(x, c, prm, past, l):
    f32 = jnp.float32
    b, L, _ = x.shape
    P = 0 if past is None else past["da_k"].shape[1]
    q_pos = P + jnp.arange(L, dtype=jnp.int32)
    k_pos = jnp.arange(P + L, dtype=jnp.int32)
    qpos_b = jnp.broadcast_to(q_pos, (b, L))

    mod = jax.nn.silu(c) @ prm["w_ada"] + prm["b_ada"]
    shift, scale, gate = jnp.split(mod, 3, axis=-1)
    h = rmsnorm(x, prm["g_pre"]) * (1 + scale[:, None]) + shift[:, None]
    u = h @ prm["w_in"]
    splits = np.cumsum(IN_SIZES)[:-1].tolist()
    (da_q, da_k, da_v, da_g, z, xbc, dt_raw,
     fx_q, fx_k, fx_v, fx_g, fx_f) = jnp.split(u, splits, axis=-1)

    da_k = da_k.reshape(b, L, DA_HEADS, DA_KDIM)
    da_v = da_v.reshape(b, L, DA_HEADS, DA_VDIM)
    k_all = da_k if past is None else jnp.concatenate([past["da_k"], da_k], axis=1)
    v_all = da_v if past is None else jnp.concatenate([past["da_v"], da_v], axis=1)
    lam_init = 0.8 - 0.6 * math.exp(-0.3 * l)
    lam = (jnp.exp(jnp.sum(prm["lq1"].astype(f32) * prm["lk1"].astype(f32)))
           - jnp.exp(jnp.sum(prm["lq2"].astype(f32) * prm["lk2"].astype(f32))) + lam_init)
    o = diff_attention(da_q.reshape(b, L, DA_HEADS, 2, DA_QK),
                       k_all.reshape(b, P + L, DA_HEADS, 2, DA_QK), v_all,
                       qpos_b, k_pos, lam, prm["rel_bias"])
    o = rmsnorm(o, prm["da_subln"]) * (1 - lam_init)
    o_da = o.reshape(b, L, DA_WIDTH) * jax.nn.silu(da_g)

    buf = jnp.zeros((b, CONV_W - 1, CONV_CH), xbc.dtype) if past is None else past["conv"]
    xp = jnp.concatenate([buf, xbc], axis=1)
    conv = prm["conv_b"] + xp[:, 0:L] * prm["conv_w"][0]
    for j in range(1, CONV_W):
        conv = conv + xp[:, j:j + L] * prm["conv_w"][j]
    new_conv = xp[:, L:]
    xbc_c = jax.nn.silu(conv)
    xs, Bm, Cm = jnp.split(xbc_c, [SSM_WIDTH, SSM_WIDTH + SSM_GROUPS * SSM_STATE], axis=-1)
    xs = xs.reshape(b, L, SSM_HEADS, SSM_HEAD_DIM).astype(f32)
    rep = SSM_HEADS // SSM_GROUPS
    Bm = jnp.repeat(Bm.reshape(b, L, SSM_GROUPS, SSM_STATE), rep, axis=2).astype(f32)
    Cm = jnp.repeat(Cm.reshape(b, L, SSM_GROUPS, SSM_STATE), rep, axis=2).astype(f32)
    dt = jax.nn.softplus((dt_raw + prm["dt_bias"]).astype(f32))
    A = -jnp.exp(prm["a_log"].astype(f32))
    init = jnp.zeros((b, SSM_HEADS, SSM_HEAD_DIM, SSM_STATE), f32) if past is None else past["ssm"]
    y, s_fin = ssd_scan(xs * dt[..., None], dt * A, Bm, Cm, init)
    y = y + prm["d_skip"].astype(f32)[:, None] * xs
    y = y.reshape(b, L, SSM_WIDTH)
    o_ssm = rmsnorm(y * jax.nn.silu(z.astype(f32)), prm["ssm_norm"]).astype(x.dtype)

    fx_k = fx_k.reshape(b, L, FOX_HEADS, FOX_HEAD_DIM)
    fx_v = fx_v.reshape(b, L, FOX_HEADS, FOX_HEAD_DIM)
    logf = jax.nn.log_sigmoid((fx_f + prm["fox_fbias"]).astype(f32))
    fk_all = fx_k if past is None else jnp.concatenate([past["fox_k"], fx_k], axis=1)
    fv_all = fx_v if past is None else jnp.concatenate([past["fox_v"], fx_v], axis=1)
    logf_all = logf if past is None else jnp.concatenate([past["fox_logf"].astype(f32), logf], axis=1)
    csum = jnp.cumsum(logf_all, axis=1)
    o = forgetting_attention(fx_q.reshape(b, L, FOX_HEADS, FOX_HEAD_DIM), fk_all, fv_all,
                             csum[:, P:], csum, qpos_b, k_pos)
    o = rmsnorm(o, prm["fox_norm"])
    o_fox = o.reshape(b, L, FOX_WIDTH) * jax.nn.silu(fx_g)

    mix = jnp.concatenate([o_da, o_ssm, o_fox], axis=-1) @ prm["w_out"]
    x = x + gate[:, None] * rmsnorm(mix, prm["g_post"])
    new = {"da_k": da_k, "da_v": da_v, "fox_k": fx_k, "fox_v": fx_v,
           "fox_logf": logf, "ssm": s_fin, "conv": new_conv}
    return x, new


def setup_inputs(seed: int = 0) -> dict:
    key = jax.random.key(seed)
    keys = jax.random.split(key, 32)
    f32 = jnp.float32

    def nrm(i, shape, s):
        return s * jax.random.normal(keys[i], shape, f32)

    def uni(i, shape, lo, hi):
        return jax.random.uniform(keys[i], shape, f32, lo, hi)

    n_pages = PAST_LEN // PAGE_SIZE
    n_phys = (DEC_BATCH * n_pages * 5) // 4
    page_table = jax.random.permutation(keys[0], n_phys)[: DEC_BATCH * n_pages]
    page_table = page_table.reshape(DEC_BATCH, n_pages).astype(jnp.int32)

    dt0 = jnp.exp(uni(26, (DEPTH, SSM_HEADS), math.log(1e-3), math.log(1e-1)))
    return {
        "x_prompt": nrm(1, (BATCH, SEQ, D_MODEL), 1.0),
        "x_sample": nrm(2, (DEC_BATCH, DEC_SEQ, D_MODEL), 1.0),
        "c_prompt": nrm(3, (BATCH, D_MODEL), 1.0),
        "c_sample": nrm(4, (DEC_BATCH, D_MODEL), 1.0),
        "cache_da_k": nrm(5, (DEPTH, n_phys, PAGE_SIZE, DA_HEADS, DA_KDIM), 1.0),
        "cache_da_v": nrm(6, (DEPTH, n_phys, PAGE_SIZE, DA_HEADS, DA_VDIM), 1.0),
        "cache_fox_k": nrm(7, (DEPTH, n_phys, PAGE_SIZE, FOX_HEADS, FOX_HEAD_DIM), 1.0),
        "cache_fox_v": nrm(8, (DEPTH, n_phys, PAGE_SIZE, FOX_HEADS, FOX_HEAD_DIM), 1.0),
        "cache_fox_logf": jax.nn.log_sigmoid(3.0 + nrm(9, (DEPTH, n_phys, PAGE_SIZE, FOX_HEADS), 1.0)),
        "state_ssm": nrm(10, (DEPTH, DEC_BATCH, SSM_HEADS, SSM_HEAD_DIM, SSM_STATE), 0.1),
        "state_conv": nrm(11, (DEPTH, DEC_BATCH, CONV_W - 1, CONV_CH), 1.0),
        "page_table": page_table,
        "rel_bias": nrm(12, (N_BUCKETS, DA_HEADS), 0.5),
        "w_ada": nrm(13, (DEPTH, D_MODEL, 3 * D_MODEL), 0.5 * D_MODEL ** -0.5),
        "b_ada": nrm(14, (DEPTH, 3 * D_MODEL), 0.02),
        "g_pre": 1.0 + nrm(15, (DEPTH, D_MODEL), 0.05),
        "g_post": 1.0 + nrm(16, (DEPTH, D_MODEL), 0.05),
        "w_in": nrm(17, (DEPTH, D_MODEL, D_IN), D_MODEL ** -0.5),
        "w_out": nrm(18, (DEPTH, D_MIX, D_MODEL), D_MIX ** -0.5),
        "da_lam_q1": nrm(19, (DEPTH, DA_QK), 0.1),
        "da_lam_k1": nrm(20, (DEPTH, DA_QK), 0.1),
        "da_lam_q2": nrm(21, (DEPTH, DA_QK), 0.1),
        "da_lam_k2": nrm(22, (DEPTH, DA_QK), 0.1),
        "da_subln": 1.0 + nrm(23, (DEPTH, DA_VDIM), 0.05),
        "conv_w": nrm(24, (DEPTH, CONV_W, CONV_CH), CONV_W ** -0.5),
        "conv_b": nrm(25, (DEPTH, CONV_CH), 0.02),
        "dt_bias": dt0 + jnp.log(-jnp.expm1(-dt0)),
        "a_log": jnp.log(uni(27, (DEPTH, SSM_HEADS), 1.0, 16.0)),
        "d_skip": 1.0 + nrm(28, (DEPTH, SSM_HEADS), 0.1),
        "ssm_norm": 1.0 + nrm(29, (DEPTH, SSM_WIDTH), 0.05),
        "fox_fbias": uni(30, (DEPTH, FOX_HEADS), 1.0, 5.0),
        "fox_norm": 1.0 + nrm(31, (DEPTH, FOX_HEAD_DIM), 0.05),
    }


def reference(x_prompt, x_sample, c_prompt, c_sample, cache_da_k, cache_da_v, cache_fox_k,
              cache_fox_v, cache_fox_logf, state_ssm, state_conv, page_table, rel_bias,
              w_ada, b_ada, g_pre, g_post, w_in, w_out, da_lam_q1, da_lam_k1, da_lam_q2,
              da_lam_k2, da_subln, conv_w, conv_b, dt_bias, a_log, d_skip, ssm_norm,
              fox_fbias, fox_norm):
    n_seq, n_pages = page_table.shape

    def gather_pages(cache, l):
        rows = cache[l, page_table]
        return rows.reshape((n_seq, n_pages * rows.shape[2]) + rows.shape[3:])

    y_p, y_s = x_prompt, x_sample
    new_p, new_s = [], []
    for l in range(DEPTH):
        prm = {"w_ada": w_ada[l], "b_ada": b_ada[l], "g_pre": g_pre[l], "g_post": g_post[l],
               "w_in": w_in[l], "w_out": w_out[l], "lq1": da_lam_q1[l], "lk1": da_lam_k1[l],
               "lq2": da_lam_q2[l], "lk2": da_lam_k2[l], "da_subln": da_subln[l],
               "rel_bias": rel_bias, "conv_w": conv_w[l], "conv_b": conv_b[l],
               "dt_bias": dt_bias[l], "a_log": a_log[l], "d_skip": d_skip[l],
               "ssm_norm": ssm_norm[l], "fox_fbias": fox_fbias[l], "fox_norm": fox_norm[l]}
        past = {"da_k": gather_pages(cache_da_k, l), "da_v": gather_pages(cache_da_v, l),
                "fox_k": gather_pages(cache_fox_k, l), "fox_v": gather_pages(cache_fox_v, l),
                "fox_logf": gather_pages(cache_fox_logf, l),
                "ssm": state_ssm[l], "conv": state_conv[l]}
        y_p, st_p = mixer_layer(y_p, c_prompt, prm, None, l)
        y_s, st_s = mixer_layer(y_s, c_sample, prm, past, l)
        new_p.append(st_p)
        new_s.append(st_s)

    def stack(states, name):
        return jnp.stack([st[name] for st in states])

    p_da_k = stack(new_p, "da_k")
    p_da_v = stack(new_p, "da_v")
    p_fox_k = stack(new_p, "fox_k")
    p_fox_v = stack(new_p, "fox_v")
    p_fox_logf = stack(new_p, "fox_logf")
    p_ssm = stack(new_p, "ssm")
    p_conv = stack(new_p, "conv")
    s_da_k = stack(new_s, "da_k")
    s_da_v = stack(new_s, "da_v")
    s_fox_k = stack(new_s, "fox_k")
    s_fox_v = stack(new_s, "fox_v")
    s_fox_logf = stack(new_s, "fox_logf")
    s_ssm = stack(new_s, "ssm")
    s_conv = stack(new_s, "conv")
    return (y_p, y_s, p_da_k, p_da_v, p_fox_k, p_fox_v, p_fox_logf, p_ssm, p_conv,
            s_da_k, s_da_v, s_fox_k, s_fox_v, s_fox_logf, s_ssm, s_conv)
```

```python
import functools
import math

import jax
import jax.numpy as jnp
from jax import lax
from jax.experimental import pallas as pl
from jax.experimental.pallas import tpu as pltpu

F32 = jnp.float32
BF16 = jnp.bfloat16
I32 = jnp.int32

EPS = 1e-6
NEG_INF = -1e30

D_MODEL = 1024
N_HEADS = 4
HEAD_W = 64
ATT_W = N_HEADS * HEAD_W
DA_QK = 32
SSM_HEADS = 8
SSM_W = 512
SSM_STATE = 128
CONV_W = 4
CONV_CH = 1024
N_BUCKETS = 32
MAX_DISTANCE = 128
PAGE = 128
LANES = 128
SUBLANES = 8
VMEM_LIMIT = 56 * 1024 * 1024

DA_SCALE = DA_QK ** -0.5
FOX_SCALE = HEAD_W ** -0.5

_SIZES = (256, 256, 256, 256, 512, 1024, 8, 256, 256, 256, 256, 4)
_OFFS = [0]
for _s in _SIZES:
    _OFFS.append(_OFFS[-1] + _s)
(C_DAQ, C_DAK, C_DAV, C_DAG, C_Z, C_XBC, C_DT, C_FXQ, C_FXK, C_FXV, C_FXG, C_FXF) = _OFFS[:12]


def _cp(sem):
    return pltpu.CompilerParams(dimension_semantics=sem, vmem_limit_bytes=VMEM_LIMIT)


def _silu(v):
    return v * (1.0 / (1.0 + jnp.exp(-v)))


def _split3(v):
    a = v.astype(BF16)
    r = v - a.astype(F32)
    b = r.astype(BF16)
    r = r - b.astype(F32)
    return a, b, r.astype(BF16)


def _dot(a, b):
    return jnp.dot(a, b, preferred_element_type=F32)


def _dot_nt(a, b):
    return lax.dot_general(a, b, (((1,), (1,)), ((), ())), preferred_element_type=F32)


def _dot_exact_rhs(v, w):
    a, b, c = _split3(v)
    return _dot(a, w) + _dot(b, w) + _dot(c, w)


def _dot_exact_lhs(w, v):
    a, b, c = _split3(v)
    return _dot(w, a) + _dot(w, b) + _dot(w, c)


def _adaln_kernel(c_ref, w_ref, b_ref, o_ref):
    c = c_ref[...]
    o_ref[...] = _dot(_silu(c), w_ref[...]) + b_ref[...]


def _adaln(c_all, w_ada, b_ada):
    depth = w_ada.shape[0]
    rows = c_all.shape[0]
    nblk = 3
    return pl.pallas_call(
        _adaln_kernel,
        grid=(depth, nblk),
        in_specs=[
            pl.BlockSpec((rows, D_MODEL), lambda l, j: (0, 0)),
            pl.BlockSpec((None, D_MODEL, D_MODEL), lambda l, j: (l, 0, j)),
            pl.BlockSpec((None, 1, D_MODEL), lambda l, j: (l, 0, j)),
        ],
        out_specs=pl.BlockSpec((None, rows, D_MODEL), lambda l, j: (l, 0, j)),
        out_shape=jax.ShapeDtypeStruct((depth, rows, 3 * D_MODEL), F32),
        compiler_params=_cp(("parallel", "parallel")),
        name="adaln",
    )(c_all, w_ada, b_ada.reshape(depth, 1, 3 * D_MODEL))


def _t5_kernel(tab_ref, rel_ref, o_ref):
    rel = rel_ref[...]
    n = jnp.maximum(rel, 0)
    max_exact = N_BUCKETS // 2
    nf = jnp.maximum(n, 1).astype(F32)
    large = max_exact + (jnp.log(nf / max_exact) / math.log(MAX_DISTANCE / max_exact)
                         * (N_BUCKETS - max_exact)).astype(I32)
    large = jnp.minimum(large, N_BUCKETS - 1)
    bucket = jnp.where(n < max_exact, n, large)
    for h in range(N_HEADS):
        acc = jnp.zeros(rel.shape, F32)
        for k in range(N_BUCKETS):
            acc = jnp.where(bucket == k, tab_ref[k, h], acc)
        o_ref[h] = jnp.where(rel >= 0, acc, NEG_INF)


def _t5_bias(rel_bias, rel, block_rows):
    rows, cols = rel.shape
    return pl.pallas_call(
        _t5_kernel,
        grid=(rows // block_rows,),
        in_specs=[
            pl.BlockSpec(memory_space=pltpu.SMEM),
            pl.BlockSpec((block_rows, cols), lambda i: (i, 0)),
        ],
        out_specs=pl.BlockSpec((N_HEADS, block_rows, cols), lambda i: (0, i, 0)),
        out_shape=jax.ShapeDtypeStruct((N_HEADS, rows, cols), F32),
        compiler_params=_cp(("parallel",)),
        name="t5_bias",
    )(rel_bias, rel)


def _in_proj_kernel(x_ref, shift_ref, scale_ref, g_ref, wr_ref, wkv_ref, ws_ref, sb_ref,
                    qda_ref, gda_ref, z_ref, xbc_ref, small_ref, qfx_ref, gfx_ref,
                    kda_ref, vda_ref, kfx_ref, vfx_ref,
                    kdab_ref, vdab_ref, kfxb_ref, vfxb_ref):
    x = x_ref[...]
    y = x * lax.rsqrt(jnp.mean(x * x, axis=-1, keepdims=True) + EPS)
    h = (y * g_ref[...]) * (1.0 + scale_ref[...]) + shift_ref[...]
    hb = h.astype(BF16)

    def rows(a, b):
        return _dot(hb, wr_ref[:, a:b])

    qda_ref[...] = (rows(0, 256) * DA_SCALE).astype(BF16)
    gda_ref[...] = rows(256, 512)
    z_ref[...] = rows(512, 1024)
    xbc_ref[...] = rows(1024, 2048)
    qfx_ref[...] = (rows(2048, 2304) * FOX_SCALE).astype(BF16)
    gfx_ref[...] = rows(2304, 2560)

    v = _dot(hb, ws_ref[...]) + sb_ref[...]
    t = jnp.log1p(jnp.exp(-jnp.abs(v)))
    lane = lax.broadcasted_iota(I32, v.shape, 1)
    small_ref[...] = jnp.where(lane < 8, jnp.maximum(v, 0.0) + t,
                               jnp.where(lane < 12, jnp.minimum(v, 0.0) - t, 0.0))

    for j, (o_ref, ob_ref) in enumerate(((kda_ref, kdab_ref), (vda_ref, vdab_ref),
                                         (kfx_ref, kfxb_ref), (vfx_ref, vfxb_ref))):
        kt = _dot_nt(wkv_ref[j * ATT_W:(j + 1) * ATT_W, :], hb)
        o_ref[...] = kt
        ob_ref[...] = kt.astype(BF16)


def _in_proj(x, shift, scale, g_pre, w_rows, w_kvt, w_small, small_bias, tm):
    bsz, seq, _ = x.shape
    nt = seq // tm
    mod_rows = shift.shape[1]
    if mod_rows == 1:
        mod_spec = pl.BlockSpec((None, 1, D_MODEL), lambda b, i: (b, 0, 0))
    else:
        mod_spec = pl.BlockSpec((None, tm, D_MODEL), lambda b, i: (b, i, 0))

    def const(shape):
        return pl.BlockSpec(shape, lambda b, i: (0,) * len(shape))

    def row_spec(width):
        return pl.BlockSpec((None, tm, width), lambda b, i: (b, i, 0))

    t_spec = pl.BlockSpec((None, ATT_W, tm), lambda b, i: (b, 0, i))
    tb_spec = pl.BlockSpec((None, None, ATT_W, tm), lambda b, i: (b, i, 0, 0))

    def row_shape(width, dt=F32):
        return jax.ShapeDtypeStruct((bsz, seq, width), dt)

    t_shape = jax.ShapeDtypeStruct((bsz, ATT_W, seq), F32)
    tb_shape = jax.ShapeDtypeStruct((bsz, nt, ATT_W, tm), BF16)
    return pl.pallas_call(
        _in_proj_kernel,
        grid=(bsz, nt),
        in_specs=[row_spec(D_MODEL), mod_spec, mod_spec, const((1, D_MODEL)),
                  const(w_rows.shape), const(w_kvt.shape), const(w_small.shape), const((1, LANES))],
        out_specs=[row_spec(256), row_spec(256), row_spec(512), row_spec(1024), row_spec(LANES),
                   row_spec(256), row_spec(256),
                   t_spec, t_spec, t_spec, t_spec, tb_spec, tb_spec, tb_spec, tb_spec],
        out_shape=[row_shape(256, BF16), row_shape(256), row_shape(512), row_shape(1024),
                   row_shape(LANES), row_shape(256, BF16), row_shape(256),
                   t_shape, t_shape, t_shape, t_shape, tb_shape, tb_shape, tb_shape, tb_shape],
        compiler_params=_cp(("parallel", "parallel")),
        name="in_proj",
    )(x, shift, scale, g_pre, w_rows, w_kvt, w_small, small_bias)


def _cumsum_kernel(x_ref, col_ref, row_ref, carry_s):
    @pl.when(pl.program_id(1) == 0)
    def _():
        carry_s[...] = jnp.zeros_like(carry_s)

    x = x_ref[...]
    t = x.shape[0]
    r = lax.broadcasted_iota(I32, (t, t), 0)
    c = lax.broadcasted_iota(I32, (t, t), 1)
    tri = jnp.where(r >= c, 1.0, 0.0).astype(BF16)
    cum = _dot_exact_lhs(tri, x) + carry_s[...]
    carry_s[...] = cum[t - 1:t, :]
    col_ref[...] = cum
    row_ref[...] = cum.T[SUBLANES:2 * SUBLANES, :]


def _cumsum(small, tm):
    bsz, seq, _ = small.shape
    return pl.pallas_call(
        _cumsum_kernel,
        grid=(bsz, seq // tm),
        in_specs=[pl.BlockSpec((None, tm, LANES), lambda b, i: (b, i, 0))],
        out_specs=[pl.BlockSpec((None, tm, LANES), lambda b, i: (b, i, 0)),
                   pl.BlockSpec((None, None, SUBLANES, tm), lambda b, i: (b, i, 0, 0))],
        out_shape=[jax.ShapeDtypeStruct((bsz, seq, LANES), F32),
                   jax.ShapeDtypeStruct((bsz, seq // tm, SUBLANES, tm), F32)],
        scratch_shapes=[pltpu.VMEM((1, LANES), F32)],
        compiler_params=_cp(("parallel", "arbitrary")),
        name="logf_cumsum",
    )(small)


def _lam(lamp_ref, lam_init):
    p = lamp_ref[...]
    s1 = jnp.sum(p[0:1] * p[1:2], axis=-1, keepdims=True)
    s2 = jnp.sum(p[2:3] * p[3:4], axis=-1, keepdims=True)
    return jnp.exp(s1) - jnp.exp(s2) + lam_init


def _head_norm(o, n_heads):
    lane = lax.broadcasted_iota(I32, o.shape, 1)
    o2 = o * o
    rs = jnp.zeros_like(o)
    for h in range(n_heads):
        m = (lane >= h * HEAD_W) & (lane < (h + 1) * HEAD_W)
        ss = jnp.sum(jnp.where(m, o2, 0.0), axis=-1, keepdims=True)
        rs = jnp.where(m, lax.rsqrt(ss / HEAD_W + EPS), rs)
    return o * rs


def _flash_kernel(*refs, mode, tile, lam_init):
    if mode == "da":
        (q_ref, k_ref, v_ref, gate_ref, nw_ref, bias_ref, lamp_ref,
         o_ref, qm_s, m_s, l_s, acc_s) = refs
        ncomp = 2
    else:
        (q_ref, k_ref, v_ref, gate_ref, nw_ref, bias_ref, ccol_ref, crow_ref,
         o_ref, qm_s, m_s, l_s, acc_s, cq_s) = refs
        ncomp = 1
    hp = pl.program_id(1)
    qi = pl.program_id(2)
    nr = 2 * ncomp
    width = HEAD_W // ncomp
    lane = lax.broadcasted_iota(I32, (tile, LANES), 1)

    q = q_ref[...]
    for r in range(nr):
        sel = (lane >= r * width) & (lane < (r + 1) * width)
        qm_s[r] = jnp.where(sel, q, jnp.zeros_like(q))
        m_s[r] = jnp.full((tile, 1), NEG_INF, F32)
        l_s[r] = jnp.zeros((tile, 1), F32)
        acc_s[r] = jnp.zeros((tile, LANES), F32)
    if mode == "fox":
        ccol = ccol_ref[...]
        for hh in range(2):
            cq_s[hh] = jnp.sum(jnp.where(lane == SUBLANES + hp * 2 + hh, ccol, 0.0),
                               axis=-1, keepdims=True)

    def body(kj, carry):
        kt = k_ref[kj]
        vt = v_ref[kj]
        kind = jnp.clip(kj - qi + 2, 0, 2)
        for r in range(nr):
            hh = r // ncomp
            s = _dot(qm_s[r], kt)
            if mode == "da":
                s = s + bias_ref[hh, kind]
                shift = 0.0
            else:
                s = s + bias_ref[0, kind] - crow_ref[kj, pl.ds(hp * 2 + hh, 1), :]
                shift = cq_s[hh]
            m_prev = m_s[r]
            m_new = jnp.maximum(m_prev, jnp.max(s, axis=-1, keepdims=True) + shift)
            alpha = jnp.exp(m_prev - m_new)
            p = jnp.exp(s - (m_new - shift))
            l_s[r] = alpha * l_s[r] + jnp.sum(p, axis=-1, keepdims=True)
            acc_s[r] = alpha * acc_s[r] + _dot_nt(p.astype(BF16), vt)
            m_s[r] = m_new
        return carry

    lax.fori_loop(0, qi + 1, body, 0)

    outs = []
    for hh in range(2):
        if mode == "da":
            lam = _lam(lamp_ref, lam_init)
            outs.append(acc_s[2 * hh] / l_s[2 * hh] - lam * (acc_s[2 * hh + 1] / l_s[2 * hh + 1]))
        else:
            outs.append(acc_s[hh] / l_s[hh])
    o = jnp.where(lane < HEAD_W, outs[0], outs[1])
    y = _head_norm(o, 2) * nw_ref[...]
    if mode == "da":
        y = y * (1.0 - lam_init)
    o_ref[...] = y * _silu(gate_ref[...])


def _flash(mode, q, kt, vt, gate, norm_w, bias, extra, tile, lam_init=0.0):
    bsz, seq, _ = q.shape
    nq = seq // tile
    ncomp = 2 if mode == "da" else 1
    nr = 2 * ncomp
    blk = pl.BlockSpec((None, tile, LANES), lambda b, p, i: (b, i, p))
    kv_spec = pl.BlockSpec((None, nq, LANES, tile), lambda b, p, i: (b, 0, p, 0))
    nw_spec = pl.BlockSpec((1, LANES), lambda b, p, i: (0, 0))
    in_specs = [blk, kv_spec, kv_spec, blk, nw_spec]
    scratch = [pltpu.VMEM((nr, tile, LANES), BF16), pltpu.VMEM((nr, tile, 1), F32),
               pltpu.VMEM((nr, tile, 1), F32), pltpu.VMEM((nr, tile, LANES), F32)]
    if mode == "da":
        in_specs += [pl.BlockSpec((2, 3, tile, tile), lambda b, p, i: (p, 0, 0, 0)),
                     pl.BlockSpec((SUBLANES, LANES), lambda b, p, i: (0, 0))]
    else:
        in_specs += [pl.BlockSpec((1, 3, tile, tile), lambda b, p, i: (0, 0, 0, 0)),
                     pl.BlockSpec((None, tile, LANES), lambda b, p, i: (b, i, 0)),
                     pl.BlockSpec((None, nq, SUBLANES, tile), lambda b, p, i: (b, 0, 0, 0))]
        scratch.append(pltpu.VMEM((2, tile, 1), F32))
    return pl.pallas_call(
        functools.partial(_flash_kernel, mode=mode, tile=tile, lam_init=lam_init),
        grid=(bsz, 2, nq),
        in_specs=in_specs,
        out_specs=blk,
        out_shape=jax.ShapeDtypeStruct((bsz, seq, ATT_W), F32),
        scratch_shapes=scratch,
        compiler_params=_cp(("parallel", "parallel", "arbitrary")),
        name="flash_" + mode,
    )(q, kt, vt, gate, norm_w, bias, *extra)


def _ssd_kernel(xbc_ref, z_ref, small_ref, cinit_ref, sinit_ref, cw_ref, cb_ref, alog_ref,
                dskip_ref, nw_ref, o_ref, sfin_ref, state_s, tail_s, *, chunk, valid_len):
    c = pl.program_id(1)
    nchunks = pl.num_programs(1)

    @pl.when(c == 0)
    def _():
        state_s[...] = sinit_ref[...]
        tail_s[...] = cinit_ref[...]

    x = xbc_ref[...]
    xcat = jnp.concatenate([tail_s[...], x], axis=0)
    conv = cb_ref[...] + xcat[SUBLANES - 3:SUBLANES - 3 + chunk] * cw_ref[0:1, :]
    for j in range(1, CONV_W):
        conv = conv + xcat[SUBLANES - 3 + j:SUBLANES - 3 + j + chunk] * cw_ref[j:j + 1, :]
    tail_s[...] = x[chunk - SUBLANES:chunk]
    xc = _silu(conv)
    xs = xc[:, :SSM_W]
    bm = xc[:, SSM_W:SSM_W + 2 * SSM_STATE].astype(BF16)
    cm = xc[:, SSM_W + 2 * SSM_STATE:].astype(BF16)

    lane = lax.broadcasted_iota(I32, (chunk, LANES), 1)
    row = lax.broadcasted_iota(I32, (chunk, LANES), 0)
    dtv = jnp.where(lane < SSM_HEADS, small_ref[...], 0.0)
    if valid_len is not None:
        dtv = jnp.where(row + c * chunk < valid_len, dtv, 0.0)
    d_a = dtv * (-jnp.exp(alog_ref[...]))

    ri = lax.broadcasted_iota(I32, (chunk, chunk), 0)
    ci = lax.broadcasted_iota(I32, (chunk, chunk), 1)
    causal = ri >= ci
    tri = jnp.where(causal, 1.0, 0.0).astype(BF16)
    a_cs = _dot_exact_lhs(tri, d_a)
    a_cs_t = a_cs.T

    er = lax.broadcasted_iota(I32, (LANES, SSM_W), 0)
    ec = lax.broadcasted_iota(I32, (LANES, SSM_W), 1)
    expand = jnp.where((ec >= er * HEAD_W) & (ec < (er + 1) * HEAD_W), 1.0, 0.0).astype(BF16)
    a_exp = _dot_exact_rhs(a_cs, expand)
    dt_exp = _dot_exact_rhs(dtv, expand)
    tot_exp = a_exp[chunk - 1:chunk, :]
    e_col = jnp.exp(a_exp)
    xdt = xs * dt_exp
    xw = xdt * jnp.exp(tot_exp - a_exp)

    srow = lax.broadcasted_iota(I32, (LANES, 1), 0)
    ys = []
    for g in range(2):
        bg = bm[:, g * SSM_STATE:(g + 1) * SSM_STATE]
        cg = cm[:, g * SSM_STATE:(g + 1) * SSM_STATE]
        scores = _dot_nt(cg, bg)
        for pp in range(2):
            p = g * 2 + pp
            sl = slice(p * LANES, (p + 1) * LANES)
            xdt_p = xdt[:, sl].astype(BF16)
            yd = []
            for h in (2 * p, 2 * p + 1):
                seg = a_cs[:, h:h + 1] - a_cs_t[h:h + 1, :]
                decay = jnp.exp(jnp.where(causal, seg, NEG_INF))
                yd.append(_dot((scores * decay).astype(BF16), xdt_p))
            y_diag = jnp.where(lane < HEAD_W, yd[0], yd[1])
            prev = state_s[p]
            y_off = _dot_nt(cg, prev.astype(BF16)) * e_col[:, sl]
            cs = _dot(xw[:, sl].T.astype(BF16), bg)
            e_tot = jnp.where(srow < HEAD_W, jnp.exp(a_cs[chunk - 1:chunk, 2 * p:2 * p + 1]),
                              jnp.exp(a_cs[chunk - 1:chunk, 2 * p + 1:2 * p + 2]))
            state_s[p] = prev * e_tot + cs
            ys.append(y_diag + y_off + dskip_ref[:, sl] * xs[:, sl])
    y = jnp.concatenate(ys, axis=1)
    yz = y * _silu(z_ref[...])
    o_ref[...] = yz * lax.rsqrt(jnp.mean(yz * yz, axis=-1, keepdims=True) + EPS) * nw_ref[...]

    @pl.when(c == nchunks - 1)
    def _():
        sfin_ref[...] = state_s[...]


def _ssd(xbc, z, small, conv_init, ssm_init, conv_w, conv_b, a_log, d_skip, norm_w, chunk, valid_len):
    bsz, seq, _ = xbc.shape

    def row_spec(width):
        return pl.BlockSpec((None, chunk, width), lambda b, c: (b, c, 0))

    def const(shape):
        return pl.BlockSpec(shape, lambda b, c: (0,) * len(shape))

    st_spec = pl.BlockSpec((None, 4, LANES, SSM_STATE), lambda b, c: (b, 0, 0, 0))
    return pl.pallas_call(
        functools.partial(_ssd_kernel, chunk=chunk, valid_len=valid_len),
        grid=(bsz, seq // chunk),
        in_specs=[row_spec(CONV_CH), row_spec(SSM_W), row_spec(LANES),
                  pl.BlockSpec((None, SUBLANES, CONV_CH), lambda b, c: (b, 0, 0)), st_spec,
                  const((SUBLANES, CONV_CH)), const((1, CONV_CH)), const((1, LANES)),
                  const((1, SSM_W)), const((1, SSM_W))],
        out_specs=[row_spec(SSM_W), st_spec],
        out_shape=[jax.ShapeDtypeStruct((bsz, seq, SSM_W), F32),
                   jax.ShapeDtypeStruct((bsz, 4, LANES, SSM_STATE), F32)],
        scratch_shapes=[pltpu.VMEM((4, LANES, SSM_STATE), F32), pltpu.VMEM((SUBLANES, CONV_CH), F32)],
        compiler_params=_cp(("parallel", "arbitrary")),
        name="ssd",
    )(xbc, z, small, conv_init, ssm_init, conv_w, conv_b, a_log, d_skip, norm_w)


def _out_proj_kernel(oda_ref, ossm_ref, ofx_ref, x_ref, gate_ref, w_ref, g_ref, y_ref):
    mix = (_dot(oda_ref[...].astype(BF16), w_ref[0:ATT_W, :])
           + _dot(ossm_ref[...].astype(BF16), w_ref[ATT_W:ATT_W + SSM_W, :])
           + _dot(ofx_ref[...].astype(BF16), w_ref[ATT_W + SSM_W:, :]))
    n = mix * lax.rsqrt(jnp.mean(mix * mix, axis=-1, keepdims=True) + EPS) * g_ref[...]
    y_ref[...] = x_ref[...] + gate_ref[...] * n


def _out_proj(o_da, o_ssm, o_fx, x, gate, w_out, g_post, tm):
    bsz, seq, _ = x.shape
    if gate.shape[1] == 1:
        gate_spec = pl.BlockSpec((None, 1, D_MODEL), lambda b, i: (b, 0, 0))
    else:
        gate_spec = pl.BlockSpec((None, tm, D_MODEL), lambda b, i: (b, i, 0))

    def row_spec(width):
        return pl.BlockSpec((None, tm, width), lambda b, i: (b, i, 0))

    return pl.pallas_call(
        _out_proj_kernel,
        grid=(bsz, seq // tm),
        in_specs=[row_spec(ATT_W), row_spec(SSM_W), row_spec(ATT_W), row_spec(D_MODEL), gate_spec,
                  pl.BlockSpec((D_MODEL, D_MODEL), lambda b, i: (0, 0)),
                  pl.BlockSpec((1, D_MODEL), lambda b, i: (0, 0))],
        out_specs=row_spec(D_MODEL),
        out_shape=jax.ShapeDtypeStruct((bsz, seq, D_MODEL), F32),
        compiler_params=_cp(("parallel", "parallel")),
        name="out_proj",
    )(o_da, o_ssm, o_fx, x, gate, w_out, g_post)


def _decode_kernel(pt_ref, *refs, mode, pages, past, lam_init):
    del pt_ref
    it = iter(refs)
    qm_ref = next(it)
    k_refs = [next(it) for _ in range(pages)]
    v_refs = [next(it) for _ in range(pages)]
    if mode == "fox":
        lf_refs = [next(it) for _ in range(pages)]
        small_ref = next(it)
    else:
        bias_ref = next(it)
        biasn_ref = next(it)
        lamp_ref = next(it)
    knew_ref = next(it)
    vnew_ref = next(it)
    gate_ref = next(it)
    nw_ref = next(it)
    o_ref = next(it)
    m_s = next(it)
    l_s = next(it)
    acc_s = next(it)
    carry_s = next(it) if mode == "fox" else None

    s_idx = pl.program_id(0)
    g = pl.program_id(1)
    ng = pl.num_programs(1)
    nrow = qm_ref.shape[0]
    span = pages * PAGE

    @pl.when(g == 0)
    def _():
        m_s[...] = jnp.full(m_s.shape, NEG_INF, F32)
        l_s[...] = jnp.zeros(l_s.shape, F32)
        acc_s[...] = jnp.zeros(acc_s.shape, F32)
        if mode == "fox":
            carry_s[...] = jnp.zeros(carry_s.shape, F32)

    qm = qm_ref[...]

    def update(s, v_list):
        m_prev = m_s[...]
        m_new = jnp.maximum(m_prev, jnp.max(s, axis=-1, keepdims=True))
        alpha = jnp.exp(m_prev - m_new)
        p = jnp.exp(s - m_new)
        l_s[...] = alpha * l_s[...] + jnp.sum(p, axis=-1, keepdims=True)
        pv = None
        for j, vt in enumerate(v_list):
            t = _dot_nt(p[:, j * PAGE:(j + 1) * PAGE].astype(BF16), vt)
            pv = t if pv is None else pv + t
        acc_s[...] = alpha * acc_s[...] + pv
        m_s[...] = m_new

    def expand_rows(ck):
        return jnp.concatenate(
            [jnp.broadcast_to(ck[h:h + 1, :], (SUBLANES, PAGE)) for h in range(N_HEADS)], axis=0)

    def upper(block):
        r = lax.broadcasted_iota(I32, (PAGE, PAGE), 0)
        c = lax.broadcasted_iota(I32, (PAGE, PAGE), 1)
        keep = r <= c
        if block:
            keep = keep & (r // block == c // block)
        return jnp.where(keep, 1.0, 0.0).astype(BF16)

    s_parts = [_dot(qm, k_refs[j][...].astype(BF16)) for j in range(pages)]
    if mode == "fox":
        lf = jnp.concatenate([lf_refs[j][...] for j in range(pages)], axis=0)
        cum = _dot_exact_rhs(lf, upper(0))
        carry = carry_s[...]
        for j in range(pages):
            cj = cum[j * SUBLANES:(j + 1) * SUBLANES]
            s_parts[j] = s_parts[j] - expand_rows(cj + carry)
            carry = carry + cj[:, PAGE - 1:PAGE]
        carry_s[...] = carry
        s = jnp.concatenate(s_parts, axis=1)
    else:
        s = jnp.concatenate(s_parts, axis=1) + bias_ref[g]
    update(s, [v_refs[j][...].astype(BF16) for j in range(pages)])

    @pl.when(g == ng - 1)
    def _():
        sn = _dot(qm, knew_ref[...].astype(BF16))
        col = lax.broadcasted_iota(I32, (nrow, PAGE), 1)
        t_row = lax.broadcasted_iota(I32, (nrow, PAGE), 0) % SUBLANES
        rel = t_row - (col - s_idx * 4)
        ok = (rel >= 0) & (col >= s_idx * 4) & (t_row < 4)
        if mode == "fox":
            lfn = small_ref[...].T[SUBLANES:2 * SUBLANES, :]
            ckn = expand_rows(_dot_exact_rhs(lfn, upper(4)) + carry_s[...])
            cq = jnp.sum(jnp.where(rel == 0, ckn, 0.0), axis=-1, keepdims=True)
            m_s[...] = m_s[...] + cq
            sn = sn - ckn + cq
        else:
            bn = biasn_ref[...]
            add = jnp.zeros_like(sn)
            for r in range(4):
                add = jnp.where(rel == r, bn[:, r:r + 1], add)
            sn = sn + add
        sn = jnp.where(ok, sn, NEG_INF)
        update(sn, [vnew_ref[...].astype(BF16)])

        o1 = acc_s[...] / l_s[...]
        if mode == "da":
            a = o1[0:4 * SUBLANES] - _lam(lamp_ref, lam_init) * o1[4 * SUBLANES:]
        else:
            a = o1
        lane = lax.broadcasted_iota(I32, (SUBLANES, ATT_W), 1)
        o8 = jnp.zeros((SUBLANES, ATT_W), F32)
        for h in range(N_HEADS):
            o8 = jnp.where((lane >= h * HEAD_W) & (lane < (h + 1) * HEAD_W),
                           a[h * SUBLANES:(h + 1) * SUBLANES], o8)
        y = _head_norm(o8, N_HEADS) * nw_ref[...]
        if mode == "da":
            y = y * (1.0 - lam_init)
        o_ref[...] = y[0:4] * _silu(gate_ref[...])


def _decode(mode, layer, page_table, qm, k_cache, v_cache, knew, vnew, gate, norm_w, extra,
            pages, lam_init=0.0):
    nseq, nrow, _ = qm.shape
    npages = page_table.shape[1]
    ng = npages // pages
    past = npages * PAGE

    def page_spec(j, shape):
        nd = len(shape)
        return pl.BlockSpec((None, None) + shape,
                            lambda s, g, pt, j=j: (layer, pt[s * npages + g * pages + j]) + (0,) * nd)

    def const(shape):
        return pl.BlockSpec(shape, lambda s, g, pt: (0,) * len(shape))

    in_specs = [pl.BlockSpec((None, nrow, ATT_W), lambda s, g, pt: (s, 0, 0))]
    args = [qm]
    in_specs += [page_spec(j, (ATT_W, PAGE)) for j in range(pages)]
    args += [k_cache] * pages
    in_specs += [page_spec(j, (ATT_W, PAGE)) for j in range(pages)]
    args += [v_cache] * pages
    scratch = [pltpu.VMEM((nrow, 1), F32), pltpu.VMEM((nrow, 1), F32), pltpu.VMEM((nrow, ATT_W), F32)]
    if mode == "fox":
        lf_cache, small = extra
        in_specs += [page_spec(j, (SUBLANES, PAGE)) for j in range(pages)]
        args += [lf_cache] * pages
        in_specs.append(const((PAGE, LANES)))
        args.append(small)
        scratch.append(pltpu.VMEM((SUBLANES, PAGE), F32))
    else:
        bias, bias_new, lamp = extra
        in_specs += [const(bias.shape), const(bias_new.shape), const(lamp.shape)]
        args += [bias, bias_new, lamp]
    in_specs += [const((ATT_W, PAGE)), const((ATT_W, PAGE)),
                 pl.BlockSpec((None, 4, ATT_W), lambda s, g, pt: (s, 0, 0)),
                 const((1, ATT_W))]
    args += [knew, vnew, gate, norm_w]
    return pl.pallas_call(
        functools.partial(_decode_kernel, mode=mode, pages=pages, past=past, lam_init=lam_init),
        grid_spec=pltpu.PrefetchScalarGridSpec(
            num_scalar_prefetch=1, grid=(nseq, ng), in_specs=in_specs,
            out_specs=pl.BlockSpec((None, 4, ATT_W), lambda s, g, pt: (s, 0, 0)),
            scratch_shapes=scratch),
        out_shape=jax.ShapeDtypeStruct((nseq, 4, ATT_W), F32),
        compiler_params=_cp(("parallel", "arbitrary")),
        name="decode_" + mode,
    )(page_table.reshape(-1), *args)


def _pad_lanes(v, width=LANES):
    return jnp.pad(v, ((0, 0), (0, width - v.shape[-1])))


def _masked_queries(q, ncomp):
    nseq = q.shape[0]
    width = HEAD_W // ncomp
    lane = jnp.arange(ATT_W)
    q8 = jnp.pad(q, ((0, 0), (0, SUBLANES - 4), (0, 0)))
    blocks = []
    for c in range(ncomp):
        for h in range(N_HEADS):
            lo = h * HEAD_W + c * width
            blocks.append(jnp.where((lane >= lo) & (lane < lo + width), q8, jnp.zeros_like(q8)))
    return jnp.concatenate(blocks, axis=1).reshape(nseq, ncomp * N_HEADS * SUBLANES, ATT_W)


def kernel(x_prompt, x_sample, c_prompt, c_sample, cache_da_k, cache_da_v, cache_fox_k, cache_fox_v,
           cache_fox_logf, state_ssm, state_conv, page_table, rel_bias, w_ada, b_ada, g_pre, g_post,
           w_in, w_out, da_lam_q1, da_lam_k1, da_lam_q2, da_lam_k2, da_subln, conv_w, conv_b, dt_bias,
           a_log, d_skip, ssm_norm, fox_fbias, fox_norm):
    depth = w_in.shape[0]
    bsz, seq, _ = x_prompt.shape
    nseq, dec_len, _ = x_sample.shape
    nphys = cache_da_k.shape[1]
    npages = page_table.shape[1]
    past = npages * PAGE
    assert dec_len == 4 and nseq * dec_len == PAGE
    tile = 512 if seq % 512 == 0 else 128
    chunk = 128
    pages = 16 if npages % 16 == 0 else npages

    def cols(a, b):
        return w_in[:, :, a:b]

    w_rows = jnp.concatenate([cols(C_DAQ, C_DAK), cols(C_DAG, C_Z), cols(C_Z, C_XBC), cols(C_XBC, C_DT),
                              cols(C_FXQ, C_FXK), cols(C_FXG, C_FXF)], axis=-1).astype(BF16)
    w_kvt = jnp.swapaxes(jnp.concatenate([cols(C_DAK, C_DAV), cols(C_DAV, C_DAG),
                                          cols(C_FXK, C_FXV), cols(C_FXV, C_FXG)], axis=-1), 1, 2).astype(BF16)
    w_small = jnp.pad(jnp.concatenate([cols(C_DT, C_FXQ), cols(C_FXF, C_FXF + 4)], axis=-1),
                      ((0, 0), (0, 0), (0, LANES - 12))).astype(BF16)
    small_bias = _pad_lanes(jnp.concatenate([dt_bias, fox_fbias], axis=-1))
    w_out_b = w_out.astype(BF16)
    lamp = jnp.pad(jnp.stack([da_lam_q1, da_lam_k1, da_lam_q2, da_lam_k2], axis=1),
                   ((0, 0), (0, SUBLANES - 4), (0, LANES - DA_QK)))
    conv_w8 = jnp.pad(conv_w, ((0, 0), (0, SUBLANES - CONV_W), (0, 0)))
    a_log_p = _pad_lanes(a_log)
    d_skip_e = jnp.repeat(d_skip, HEAD_W, axis=-1)
    da_nw2 = jnp.tile(da_subln, (1, 2))
    da_nw4 = jnp.tile(da_subln, (1, 4))
    fx_nw2 = jnp.tile(fox_norm, (1, 2))
    fx_nw4 = jnp.tile(fox_norm, (1, 4))

    def pages_t(c):
        return jnp.transpose(c, (0, 1, 3, 4, 2)).reshape(depth, nphys, ATT_W, PAGE)

    kc_da, vc_da, kc_fx, vc_fx = (pages_t(c) for c in (cache_da_k, cache_da_v, cache_fox_k, cache_fox_v))
    lf_cache = jnp.pad(jnp.transpose(cache_fox_logf, (0, 1, 3, 2)),
                       ((0, 0), (0, 0), (0, SUBLANES - N_HEADS), (0, 0)))
    conv_init_s = jnp.pad(state_conv, ((0, 0), (0, 0), (SUBLANES - 3, 0), (0, 0)))
    ssm_init_s = state_ssm.reshape(depth, nseq, 4, LANES, SSM_STATE)
    conv_init_p = jnp.zeros((bsz, SUBLANES, CONV_CH), F32)
    ssm_init_p = jnp.zeros((bsz, 4, LANES, SSM_STATE), F32)

    nc = bsz + nseq
    nc_pad = -(-nc // SUBLANES) * SUBLANES
    c_all = jnp.pad(jnp.concatenate([c_prompt, c_sample], axis=0), ((0, nc_pad - nc), (0, 0)))
    mod = _adaln(c_all, w_ada, b_ada)

    ii = jnp.arange(tile, dtype=I32)
    d = ii[:, None] - ii[None, :]
    rel_tiles = jnp.concatenate([d + 2 * tile, d + tile, d], axis=0)
    bias_tiles = _t5_bias(rel_bias, rel_tiles, tile).reshape(N_HEADS, 3, tile, tile)
    tt = jnp.arange(SUBLANES, dtype=I32)
    rel_dec = jnp.where(tt[:, None] < dec_len, past + tt[:, None] - jnp.arange(past, dtype=I32)[None, :], 1)
    bias_dec = _t5_bias(rel_bias, rel_dec, SUBLANES)
    bias_dec = jnp.tile(bias_dec.reshape(N_HEADS * SUBLANES, past), (2, 1))
    bias_dec = jnp.transpose(bias_dec.reshape(2 * N_HEADS * SUBLANES, npages // pages, pages * PAGE), (1, 0, 2))
    bias_new = jnp.tile(jnp.repeat(_pad_lanes(rel_bias[:4].T), SUBLANES, axis=0), (2, 1))
    fox_mask = jnp.stack([jnp.zeros((tile, tile), F32), jnp.zeros((tile, tile), F32),
                          jnp.where(d >= 0, 0.0, NEG_INF).astype(F32)])[None]

    xp = x_prompt
    xs = x_sample.reshape(1, nseq * dec_len, D_MODEL)
    outs_p = [[] for _ in range(7)]
    outs_s = [[] for _ in range(7)]
    for l in range(depth):
        lam_init = 0.8 - 0.6 * math.exp(-0.3 * l)
        shift_p, scale_p, gate_p = (mod[l, :bsz, i * D_MODEL:(i + 1) * D_MODEL][:, None] for i in range(3))
        shift_s, scale_s, gate_s = (
            jnp.repeat(mod[l, bsz:nc, i * D_MODEL:(i + 1) * D_MODEL], dec_len, axis=0)[None] for i in range(3))
        gp = g_pre[l][None]
        sb = small_bias[l][None]

        (qda, gda, z, xbc, small, qfx, gfx, kda, vda, kfx, vfx, kdab, vdab, kfxb, vfxb) = _in_proj(
            xp, shift_p, scale_p, gp, w_rows[l], w_kvt[l], w_small[l], sb, tile)
        ccol, crow = _cumsum(small, tile)
        o_da = _flash("da", qda, kdab, vdab, gda, da_nw2[l][None], bias_tiles, (lamp[l],), tile, lam_init)
        o_fx = _flash("fox", qfx, kfxb, vfxb, gfx, fx_nw2[l][None], fox_mask, (ccol, crow), tile)
        o_ssm, ssm_fin = _ssd(xbc, z, small, conv_init_p, ssm_init_p, conv_w8[l], conv_b[l][None],
                              a_log_p[l][None], d_skip_e[l][None], ssm_norm[l][None], chunk, None)
        xp = _out_proj(o_da, o_ssm, o_fx, xp, gate_p, w_out_b[l], g_post[l][None], tile)

        def heads_t(a, n=bsz, s=seq):
            return jnp.transpose(a.reshape(n, N_HEADS, HEAD_W, s), (0, 3, 1, 2))

        for lst, val in zip(outs_p, (heads_t(kda), heads_t(vda), heads_t(kfx), heads_t(vfx),
                                     small[:, :, 8:12], ssm_fin.reshape(bsz, SSM_HEADS, HEAD_W, SSM_STATE),
                                     xbc[:, seq - 3:, :])):
            lst.append(val)

        (qda, gda, z, xbc, small, qfx, gfx, kda, vda, kfx, vfx, _, _, _, _) = _in_proj(
            xs, shift_s, scale_s, gp, w_rows[l], w_kvt[l], w_small[l], sb, PAGE)
        qm_da = _masked_queries(qda.reshape(nseq, dec_len, ATT_W), 2)
        qm_fx = _masked_queries(qfx.reshape(nseq, dec_len, ATT_W), 1)
        o_da = _decode("da", l, page_table, qm_da, kc_da, vc_da, kda[0], vda[0],
                       gda.reshape(nseq, dec_len, ATT_W), da_nw4[l][None],
                       (bias_dec, bias_new, lamp[l]), pages, lam_init)
        o_fx = _decode("fox", l, page_table, qm_fx, kc_fx, vc_fx, kfx[0], vfx[0],
                       gfx.reshape(nseq, dec_len, ATT_W), fx_nw4[l][None],
                       (lf_cache, small[0]), pages)

        def pad_seq(a):
            a = a.reshape(nseq, dec_len, a.shape[-1])
            return jnp.pad(a, ((0, 0), (0, chunk - dec_len), (0, 0)))

        o_ssm, ssm_fin = _ssd(pad_seq(xbc), pad_seq(z), pad_seq(small), conv_init_s[l], ssm_init_s[l],
                              conv_w8[l], conv_b[l][None], a_log_p[l][None], d_skip_e[l][None],
                              ssm_norm[l][None], chunk, dec_len)
        o_ssm = o_ssm[:, :dec_len].reshape(1, nseq * dec_len, SSM_W)
        xs = _out_proj(o_da.reshape(1, nseq * dec_len, ATT_W), o_ssm, o_fx.reshape(1, nseq * dec_len, ATT_W),
                       xs, gate_s, w_out_b[l], g_post[l][None], PAGE)

        def heads_s(a):
            return heads_t(a, 1, nseq * dec_len).reshape(nseq, dec_len, N_HEADS, HEAD_W)

        xbc_s = xbc.reshape(nseq, dec_len, CONV_CH)
        for lst, val in zip(outs_s, (heads_s(kda), heads_s(vda), heads_s(kfx), heads_s(vfx),
                                     small.reshape(nseq, dec_len, LANES)[:, :, 8:12],
                                     ssm_fin.reshape(nseq, SSM_HEADS, HEAD_W, SSM_STATE),
                                     xbc_s[:, dec_len - 3:, :])):
            lst.append(val)

    y_s = xs.reshape(nseq, dec_len, D_MODEL)
    return (xp, y_s) + tuple(jnp.stack(v) for v in outs_p) + tuple(jnp.stack(v) for v in outs_s)
```

```python
import functools
import math

import jax
import jax.numpy as jnp
from jax import lax
from jax.experimental import pallas as pl
from jax.experimental.pallas import tpu as pltpu

F32 = jnp.float32
BF16 = jnp.bfloat16
I32 = jnp.int32

EPS = 1e-6
NEG_INF = -1e30

D_MODEL = 1024
N_HEADS = 4
HEAD_W = 64
ATT_W = N_HEADS * HEAD_W
DA_QK = 32
SSM_HEADS = 8
SSM_W = 512
SSM_STATE = 128
CONV_W = 4
CONV_CH = 1024
N_BUCKETS = 32
MAX_DISTANCE = 128
PAGE = 128
LANES = 128
SUBLANES = 8
VMEM_LIMIT = 56 * 1024 * 1024

DA_SCALE = DA_QK ** -0.5
FOX_SCALE = HEAD_W ** -0.5
LOG2E = math.log2(math.e)

_SIZES = (256, 256, 256, 256, 512, 1024, 8, 256, 256, 256, 256, 4)
_OFFS = [0]
for _s in _SIZES:
    _OFFS.append(_OFFS[-1] + _s)
(C_DAQ, C_DAK, C_DAV, C_DAG, C_Z, C_XBC, C_DT, C_FXQ, C_FXK, C_FXV, C_FXG, C_FXF) = _OFFS[:12]


def _cp(sem):
    return pltpu.CompilerParams(dimension_semantics=sem, vmem_limit_bytes=VMEM_LIMIT)


def _silu(v):
    return v * (1.0 / (1.0 + jnp.exp(-v)))


def _split3(v):
    a = v.astype(BF16)
    r = v - a.astype(F32)
    b = r.astype(BF16)
    r = r - b.astype(F32)
    return a, b, r.astype(BF16)


def _dot(a, b):
    return jnp.dot(a, b, preferred_element_type=F32)


def _dot_nt(a, b):
    return lax.dot_general(a, b, (((1,), (1,)), ((), ())), preferred_element_type=F32)


def _dot_exact_rhs(v, w):
    a, b, c = _split3(v)
    return _dot(a, w) + _dot(b, w) + _dot(c, w)


def _dot_exact_lhs(w, v):
    a, b, c = _split3(v)
    return _dot(w, a) + _dot(w, b) + _dot(w, c)


def _adaln_kernel(c_ref, w_ref, b_ref, o_ref):
    c = c_ref[...]
    o_ref[...] = _dot(_silu(c), w_ref[...]) + b_ref[...]


def _adaln(c_all, w_ada, b_ada):
    depth = w_ada.shape[0]
    rows = c_all.shape[0]
    nblk = 3
    return pl.pallas_call(
        _adaln_kernel,
        grid=(depth, nblk),
        in_specs=[
            pl.BlockSpec((rows, D_MODEL), lambda l, j: (0, 0)),
            pl.BlockSpec((None, D_MODEL, D_MODEL), lambda l, j: (l, 0, j)),
            pl.BlockSpec((None, 1, D_MODEL), lambda l, j: (l, 0, j)),
        ],
        out_specs=pl.BlockSpec((None, rows, D_MODEL), lambda l, j: (l, 0, j)),
        out_shape=jax.ShapeDtypeStruct((depth, rows, 3 * D_MODEL), F32),
        compiler_params=_cp(("parallel", "parallel")),
        name="adaln",
    )(c_all, w_ada, b_ada.reshape(depth, 1, 3 * D_MODEL))


def _t5_kernel(tab_ref, rel_ref, o_ref):
    rel = rel_ref[...]
    n = jnp.maximum(rel, 0)
    max_exact = N_BUCKETS // 2
    nf = jnp.maximum(n, 1).astype(F32)
    large = max_exact + (jnp.log(nf / max_exact) / math.log(MAX_DISTANCE / max_exact)
                         * (N_BUCKETS - max_exact)).astype(I32)
    large = jnp.minimum(large, N_BUCKETS - 1)
    bucket = jnp.where(n < max_exact, n, large)
    for h in range(N_HEADS):
        acc = jnp.zeros(rel.shape, F32)
        for k in range(N_BUCKETS):
            acc = jnp.where(bucket == k, tab_ref[k, h], acc)
        o_ref[h] = jnp.where(rel >= 0, acc, NEG_INF)


def _t5_bias(rel_bias, rel, block_rows):
    rows, cols = rel.shape
    return pl.pallas_call(
        _t5_kernel,
        grid=(rows // block_rows,),
        in_specs=[
            pl.BlockSpec(memory_space=pltpu.SMEM),
            pl.BlockSpec((block_rows, cols), lambda i: (i, 0)),
        ],
        out_specs=pl.BlockSpec((N_HEADS, block_rows, cols), lambda i: (0, i, 0)),
        out_shape=jax.ShapeDtypeStruct((N_HEADS, rows, cols), F32),
        compiler_params=_cp(("parallel",)),
        name="t5_bias",
    )(rel_bias, rel)


def _in_proj_kernel(x_ref, shift_ref, scale_ref, g_ref, wr_ref, wt_ref, ws_ref, sb_ref,
                    kda_ref, gda_ref, z_ref, xbc_ref, small_ref, kfx_ref, gfx_ref,
                    qdat_ref, kdat_ref, vdat_ref, vdatb_ref, qfxt_ref, kfxt_ref, vfxt_ref, vfxtb_ref):
    x = x_ref[...]
    y = x * lax.rsqrt(jnp.mean(x * x, axis=-1, keepdims=True) + EPS)
    h = (y * g_ref[...]) * (1.0 + scale_ref[...]) + shift_ref[...]
    hb = h.astype(BF16)

    def rows(a, b):
        return _dot(hb, wr_ref[:, a:b])

    kda_ref[...] = rows(0, 256).astype(BF16)
    gda_ref[...] = rows(256, 512)
    z_ref[...] = rows(512, 1024)
    xbc_ref[...] = rows(1024, 2048)
    kfx_ref[...] = rows(2048, 2304).astype(BF16)
    gfx_ref[...] = rows(2304, 2560)

    v = _dot(hb, ws_ref[...]) + sb_ref[...]
    t = jnp.log1p(jnp.exp(-jnp.abs(v)))
    lane = lax.broadcasted_iota(I32, v.shape, 1)
    small_ref[...] = jnp.where(lane < 8, jnp.maximum(v, 0.0) + t,
                               jnp.where(lane < 12, jnp.minimum(v, 0.0) - t, 0.0))

    def cols(j):
        return _dot_nt(wt_ref[j * ATT_W:(j + 1) * ATT_W, :], hb)

    qdat_ref[...] = (cols(0) * (DA_SCALE * LOG2E)).astype(BF16)
    kdat_ref[...] = cols(1)
    v = cols(2)
    vdat_ref[...] = v
    vdatb_ref[...] = v.astype(BF16)
    qfxt_ref[...] = (cols(3) * (FOX_SCALE * LOG2E)).astype(BF16)
    kfxt_ref[...] = cols(4)
    v = cols(5)
    vfxt_ref[...] = v
    vfxtb_ref[...] = v.astype(BF16)


def _in_proj(x, shift, scale, g_pre, w_rows, w_kvt, w_small, small_bias, tm):
    bsz, seq, _ = x.shape
    nt = seq // tm
    mod_rows = shift.shape[1]
    if mod_rows == 1:
        mod_spec = pl.BlockSpec((None, 1, D_MODEL), lambda b, i: (b, 0, 0))
    else:
        mod_spec = pl.BlockSpec((None, tm, D_MODEL), lambda b, i: (b, i, 0))

    def const(shape):
        return pl.BlockSpec(shape, lambda b, i: (0,) * len(shape))

    def row_spec(width):
        return pl.BlockSpec((None, tm, width), lambda b, i: (b, i, 0))

    t_spec = pl.BlockSpec((None, ATT_W, tm), lambda b, i: (b, 0, i))
    tb_spec = pl.BlockSpec((None, None, ATT_W, tm), lambda b, i: (b, i, 0, 0))

    def row_shape(width, dt=F32):
        return jax.ShapeDtypeStruct((bsz, seq, width), dt)

    t_shape = jax.ShapeDtypeStruct((bsz, ATT_W, seq), F32)
    tq_shape = jax.ShapeDtypeStruct((bsz, ATT_W, seq), BF16)
    tb_shape = jax.ShapeDtypeStruct((bsz, nt, ATT_W, tm), BF16)
    return pl.pallas_call(
        _in_proj_kernel,
        grid=(bsz, nt),
        in_specs=[row_spec(D_MODEL), mod_spec, mod_spec, const((1, D_MODEL)),
                  const(w_rows.shape), const(w_kvt.shape), const(w_small.shape), const((1, LANES))],
        out_specs=[row_spec(256), row_spec(256), row_spec(512), row_spec(1024), row_spec(LANES),
                   row_spec(256), row_spec(256),
                   t_spec, t_spec, t_spec, tb_spec, t_spec, t_spec, t_spec, tb_spec],
        out_shape=[row_shape(256, BF16), row_shape(256), row_shape(512), row_shape(1024),
                   row_shape(LANES), row_shape(256, BF16), row_shape(256),
                   tq_shape, t_shape, t_shape, tb_shape, tq_shape, t_shape, t_shape, tb_shape],
        compiler_params=_cp(("parallel", "parallel")),
        name="in_proj",
    )(x, shift, scale, g_pre, w_rows, w_kvt, w_small, small_bias)


def _cumsum_kernel(x_ref, rep_ref, row_ref, carry_s):
    @pl.when(pl.program_id(1) == 0)
    def _():
        carry_s[...] = jnp.zeros_like(carry_s)

    x = x_ref[...]
    t = x.shape[0]
    r = lax.broadcasted_iota(I32, (t, t), 0)
    c = lax.broadcasted_iota(I32, (t, t), 1)
    tri = jnp.where(r >= c, 1.0, 0.0).astype(BF16)
    cum = _dot_exact_lhs(tri, x) + carry_s[...]
    carry_s[...] = cum[t - 1:t, :]
    sr = lax.broadcasted_iota(I32, (LANES, N_HEADS * LANES), 0)
    sc = lax.broadcasted_iota(I32, (LANES, N_HEADS * LANES), 1)
    spread = jnp.where(sr == SUBLANES + sc // LANES, 1.0, 0.0).astype(BF16)
    rep_ref[...] = _dot_exact_rhs(cum, spread) * LOG2E
    row_ref[...] = cum.T[SUBLANES:2 * SUBLANES, :] * LOG2E


def _cumsum(small, tm):
    bsz, seq, _ = small.shape
    return pl.pallas_call(
        _cumsum_kernel,
        grid=(bsz, seq // tm),
        in_specs=[pl.BlockSpec((None, tm, LANES), lambda b, i: (b, i, 0))],
        out_specs=[pl.BlockSpec((None, tm, N_HEADS * LANES), lambda b, i: (b, i, 0)),
                   pl.BlockSpec((None, None, SUBLANES, tm), lambda b, i: (b, i, 0, 0))],
        out_shape=[jax.ShapeDtypeStruct((bsz, seq, N_HEADS * LANES), F32),
                   jax.ShapeDtypeStruct((bsz, seq // tm, SUBLANES, tm), F32)],
        scratch_shapes=[pltpu.VMEM((1, LANES), F32)],
        compiler_params=_cp(("parallel", "arbitrary")),
        name="logf_cumsum",
    )(small)


def _lam(lamp_ref, lam_init):
    p = lamp_ref[...]
    s1 = jnp.sum(p[0:1] * p[1:2], axis=-1, keepdims=True)
    s2 = jnp.sum(p[2:3] * p[3:4], axis=-1, keepdims=True)
    return jnp.exp(s1) - jnp.exp(s2) + lam_init


def _head_norm(o, n_heads):
    lane = lax.broadcasted_iota(I32, o.shape, 1)
    o2 = o * o
    rs = jnp.zeros_like(o)
    for h in range(n_heads):
        m = (lane >= h * HEAD_W) & (lane < (h + 1) * HEAD_W)
        ss = jnp.sum(jnp.where(m, o2, 0.0), axis=-1, keepdims=True)
        rs = jnp.where(m, lax.rsqrt(ss / HEAD_W + EPS), rs)
    return o * rs


def _flash_kernel(*refs, mode, tile, lam_init):
    if mode == "da":
        (qt_ref, k_ref, v_ref, gate_ref, nw_ref, bias_ref, lamp_ref,
         o_ref, qm_s, m_s, mb_s, acc_s, s_s) = refs
        ncomp = 2
    else:
        (qt_ref, k_ref, v_ref, gate_ref, nw_ref, bias_ref, ckrep_ref, cqrow_ref,
         o_ref, qm_s, m_s, mb_s, acc_s, s_s, cq_s) = refs
        ncomp = 1
    hp = pl.program_id(1)
    qi = pl.program_id(2)
    nr = 2 * ncomp
    width = HEAD_W // ncomp
    row = lax.broadcasted_iota(I32, (LANES, tile), 0)

    qt = qt_ref[...]
    for r in range(nr):
        sel = (row >= r * width) & (row < (r + 1) * width)
        qm_s[r] = jnp.where(sel, qt, jnp.zeros_like(qt))
        m_s[r] = jnp.full((1, tile), NEG_INF, F32)
        acc_s[r] = jnp.zeros((LANES, tile), F32)
    if mode == "fox":
        for hh in range(2):
            cq_s[hh] = cqrow_ref[pl.ds(hp * 2 + hh, 1), :]

    def stage1(kj, slot):
        ks = pl.multiple_of(kj * tile, tile)
        kt = k_ref[pl.ds(ks, tile), :]
        kind = jnp.clip(kj - qi + 2, 0, 2)
        for r in range(nr):
            hh = r // ncomp
            s = _dot(kt, qm_s[r])
            if mode == "da":
                s = s + bias_ref[hh, kind]
                shift = 0.0
            else:
                ck = ckrep_ref[pl.ds(ks, tile), hh * LANES:(hh + 1) * LANES]
                s = s + bias_ref[0, kind] - jnp.tile(ck, (1, tile // LANES))
                shift = cq_s[hh]
            m_prev = m_s[r]
            m_new = jnp.maximum(m_prev, jnp.max(s, axis=0, keepdims=True) + shift)
            s_s[slot, r] = s
            mb_s[r] = m_prev
            m_s[r] = m_new

    def stage2_inputs():
        return [(mb_s[r], m_s[r]) for r in range(nr)]

    def stage2(kj, slot, pend):
        vt = v_ref[kj]
        for r in range(nr):
            hh = r // ncomp
            m_prev, m_new = pend[r]
            shift = cq_s[hh] if mode == "fox" else 0.0
            alpha = jnp.exp2(m_prev - m_new)
            p = jnp.exp2(s_s[slot, r] - (m_new - shift)).astype(BF16)
            own = (row >= hh * HEAD_W) & (row < (hh + 1) * HEAD_W)
            vmod = jnp.where(own, vt, jnp.ones_like(vt))
            acc_s[r] = alpha * acc_s[r] + _dot(vmod, p)

    def body(kj, carry):
        for par in range(2):
            @pl.when(kj % 2 == par)
            def _():
                pend = stage2_inputs()
                stage1(kj, par)
                stage2(kj - 1, 1 - par, pend)
        return carry

    stage1(0, 0)
    lax.fori_loop(1, qi + 1, body, 0)
    for par in range(2):
        @pl.when(qi % 2 == par)
        def _():
            stage2(qi, par, stage2_inputs())

    outs = []
    for hh in range(2):
        den = HEAD_W * (1 - hh)
        if mode == "da":
            a0 = acc_s[2 * hh]
            a1 = acc_s[2 * hh + 1]
            outs.append(a0 / a0[den:den + 1, :] - _lam(lamp_ref, lam_init) * (a1 / a1[den:den + 1, :]))
        else:
            a0 = acc_s[hh]
            outs.append(a0 / a0[den:den + 1, :])
    o = jnp.where(row < HEAD_W, outs[0], outs[1]).T
    y = _head_norm(o, 2) * nw_ref[...]
    if mode == "da":
        y = y * (1.0 - lam_init)
    o_ref[...] = y * _silu(gate_ref[...])


def _flash(mode, qt, k, vt, gate, norm_w, bias, extra, tile, lam_init=0.0):
    bsz, seq, _ = k.shape
    nq = seq // tile
    ncomp = 2 if mode == "da" else 1
    nr = 2 * ncomp
    blk = pl.BlockSpec((None, tile, LANES), lambda b, p, i: (b, i, p))
    qt_spec = pl.BlockSpec((None, LANES, tile), lambda b, p, i: (b, p, i))
    k_spec = pl.BlockSpec((None, seq, LANES), lambda b, p, i: (b, 0, p))
    v_spec = pl.BlockSpec((None, nq, LANES, tile), lambda b, p, i: (b, 0, p, 0))
    nw_spec = pl.BlockSpec((1, LANES), lambda b, p, i: (0, 0))
    in_specs = [qt_spec, k_spec, v_spec, blk, nw_spec]
    scratch = [pltpu.VMEM((nr, LANES, tile), BF16), pltpu.VMEM((nr, 1, tile), F32),
               pltpu.VMEM((nr, 1, tile), F32), pltpu.VMEM((nr, LANES, tile), F32),
               pltpu.VMEM((2, nr, tile, tile), F32)]
    if mode == "da":
        in_specs += [pl.BlockSpec((2, 3, tile, tile), lambda b, p, i: (p, 0, 0, 0)),
                     pl.BlockSpec((SUBLANES, LANES), lambda b, p, i: (0, 0))]
    else:
        in_specs += [pl.BlockSpec((1, 3, tile, tile), lambda b, p, i: (0, 0, 0, 0)),
                     pl.BlockSpec((None, seq, 2 * LANES), lambda b, p, i: (b, 0, p)),
                     pl.BlockSpec((None, None, SUBLANES, tile), lambda b, p, i: (b, i, 0, 0))]
        scratch.append(pltpu.VMEM((2, 1, tile), F32))
    return pl.pallas_call(
        functools.partial(_flash_kernel, mode=mode, tile=tile, lam_init=lam_init),
        grid=(bsz, 2, nq),
        in_specs=in_specs,
        out_specs=blk,
        out_shape=jax.ShapeDtypeStruct((bsz, seq, ATT_W), F32),
        scratch_shapes=scratch,
        compiler_params=_cp(("parallel", "parallel", "arbitrary")),
        name="flash_" + mode,
    )(qt, k, vt, gate, norm_w, bias, *extra)


def _ssd_kernel(xbc_ref, z_ref, small_ref, cinit_ref, sinit_ref, cw_ref, cb_ref, alog_ref,
                dskip_ref, nw_ref, o_ref, sfin_ref, state_s, tail_s, *, chunk, valid_len):
    c = pl.program_id(1)
    nchunks = pl.num_programs(1)

    @pl.when(c == 0)
    def _():
        state_s[...] = sinit_ref[...]
        tail_s[...] = cinit_ref[...]

    x = xbc_ref[...]
    xcat = jnp.concatenate([tail_s[...], x], axis=0)
    conv = cb_ref[...] + xcat[SUBLANES - 3:SUBLANES - 3 + chunk] * cw_ref[0:1, :]
    for j in range(1, CONV_W):
        conv = conv + xcat[SUBLANES - 3 + j:SUBLANES - 3 + j + chunk] * cw_ref[j:j + 1, :]
    tail_s[...] = x[chunk - SUBLANES:chunk]
    xc = _silu(conv)
    xs = xc[:, :SSM_W]
    bm = xc[:, SSM_W:SSM_W + 2 * SSM_STATE].astype(BF16)
    cm = xc[:, SSM_W + 2 * SSM_STATE:].astype(BF16)

    lane = lax.broadcasted_iota(I32, (chunk, LANES), 1)
    row = lax.broadcasted_iota(I32, (chunk, LANES), 0)
    dtv = jnp.where(lane < SSM_HEADS, small_ref[...], 0.0)
    if valid_len is not None:
        dtv = jnp.where(row + c * chunk < valid_len, dtv, 0.0)
    d_a = dtv * (-jnp.exp(alog_ref[...]))

    ri = lax.broadcasted_iota(I32, (chunk, chunk), 0)
    ci = lax.broadcasted_iota(I32, (chunk, chunk), 1)
    causal = ri >= ci
    tri = jnp.where(causal, 1.0, 0.0).astype(BF16)
    a_cs = _dot_exact_lhs(tri, d_a)
    a_cs_t = a_cs.T

    er = lax.broadcasted_iota(I32, (LANES, SSM_W), 0)
    ec = lax.broadcasted_iota(I32, (LANES, SSM_W), 1)
    expand = jnp.where((ec >= er * HEAD_W) & (ec < (er + 1) * HEAD_W), 1.0, 0.0).astype(BF16)
    a_exp = _dot_exact_rhs(a_cs, expand)
    dt_exp = _dot_exact_rhs(dtv, expand)
    tot_exp = a_exp[chunk - 1:chunk, :]
    e_col = jnp.exp(a_exp)
    xdt = xs * dt_exp
    xw = xdt * jnp.exp(tot_exp - a_exp)

    srow = lax.broadcasted_iota(I32, (LANES, 1), 0)
    ys = []
    for g in range(2):
        bg = bm[:, g * SSM_STATE:(g + 1) * SSM_STATE]
        cg = cm[:, g * SSM_STATE:(g + 1) * SSM_STATE]
        scores = _dot_nt(cg, bg)
        for pp in range(2):
            p = g * 2 + pp
            sl = slice(p * LANES, (p + 1) * LANES)
            xdt_p = xdt[:, sl].astype(BF16)
            yd = []
            for h in (2 * p, 2 * p + 1):
                seg = a_cs[:, h:h + 1] - a_cs_t[h:h + 1, :]
                decay = jnp.exp(jnp.where(causal, seg, NEG_INF))
                yd.append(_dot((scores * decay).astype(BF16), xdt_p))
            y_diag = jnp.where(lane < HEAD_W, yd[0], yd[1])
            prev = state_s[p]
            y_off = _dot_nt(cg, prev.astype(BF16)) * e_col[:, sl]
            cs = _dot(xw[:, sl].T.astype(BF16), bg)
            e_tot = jnp.where(srow < HEAD_W, jnp.exp(a_cs[chunk - 1:chunk, 2 * p:2 * p + 1]),
                              jnp.exp(a_cs[chunk - 1:chunk, 2 * p + 1:2 * p + 2]))
            state_s[p] = prev * e_tot + cs
            ys.append(y_diag + y_off + dskip_ref[:, sl] * xs[:, sl])
    y = jnp.concatenate(ys, axis=1)
    yz = y * _silu(z_ref[...])
    o_ref[...] = yz * lax.rsqrt(jnp.mean(yz * yz, axis=-1, keepdims=True) + EPS) * nw_ref[...]

    @pl.when(c == nchunks - 1)
    def _():
        sfin_ref[...] = state_s[...]


def _ssd(xbc, z, small, conv_init, ssm_init, conv_w, conv_b, a_log, d_skip, norm_w, chunk, valid_len):
    bsz, seq, _ = xbc.shape

    def row_spec(width):
        return pl.BlockSpec((None, chunk, width), lambda b, c: (b, c, 0))

    def const(shape):
        return pl.BlockSpec(shape, lambda b, c: (0,) * len(shape))

    st_spec = pl.BlockSpec((None, 4, LANES, SSM_STATE), lambda b, c: (b, 0, 0, 0))
    return pl.pallas_call(
        functools.partial(_ssd_kernel, chunk=chunk, valid_len=valid_len),
        grid=(bsz, seq // chunk),
        in_specs=[row_spec(CONV_CH), row_spec(SSM_W), row_spec(LANES),
                  pl.BlockSpec((None, SUBLANES, CONV_CH), lambda b, c: (b, 0, 0)), st_spec,
                  const((SUBLANES, CONV_CH)), const((1, CONV_CH)), const((1, LANES)),
                  const((1, SSM_W)), const((1, SSM_W))],
        out_specs=[row_spec(SSM_W), st_spec],
        out_shape=[jax.ShapeDtypeStruct((bsz, seq, SSM_W), F32),
                   jax.ShapeDtypeStruct((bsz, 4, LANES, SSM_STATE), F32)],
        scratch_shapes=[pltpu.VMEM((4, LANES, SSM_STATE), F32), pltpu.VMEM((SUBLANES, CONV_CH), F32)],
        compiler_params=_cp(("parallel", "arbitrary")),
        name="ssd",
    )(xbc, z, small, conv_init, ssm_init, conv_w, conv_b, a_log, d_skip, norm_w)


def _out_proj_kernel(oda_ref, ossm_ref, ofx_ref, x_ref, gate_ref, w_ref, g_ref, y_ref):
    mix = (_dot(oda_ref[...].astype(BF16), w_ref[0:ATT_W, :])
           + _dot(ossm_ref[...].astype(BF16), w_ref[ATT_W:ATT_W + SSM_W, :])
           + _dot(ofx_ref[...].astype(BF16), w_ref[ATT_W + SSM_W:, :]))
    n = mix * lax.rsqrt(jnp.mean(mix * mix, axis=-1, keepdims=True) + EPS) * g_ref[...]
    y_ref[...] = x_ref[...] + gate_ref[...] * n


def _out_proj(o_da, o_ssm, o_fx, x, gate, w_out, g_post, tm):
    bsz, seq, _ = x.shape
    if gate.shape[1] == 1:
        gate_spec = pl.BlockSpec((None, 1, D_MODEL), lambda b, i: (b, 0, 0))
    else:
        gate_spec = pl.BlockSpec((None, tm, D_MODEL), lambda b, i: (b, i, 0))

    def row_spec(width):
        return pl.BlockSpec((None, tm, width), lambda b, i: (b, i, 0))

    return pl.pallas_call(
        _out_proj_kernel,
        grid=(bsz, seq // tm),
        in_specs=[row_spec(ATT_W), row_spec(SSM_W), row_spec(ATT_W), row_spec(D_MODEL), gate_spec,
                  pl.BlockSpec((D_MODEL, D_MODEL), lambda b, i: (0, 0)),
                  pl.BlockSpec((1, D_MODEL), lambda b, i: (0, 0))],
        out_specs=row_spec(D_MODEL),
        out_shape=jax.ShapeDtypeStruct((bsz, seq, D_MODEL), F32),
        compiler_params=_cp(("parallel", "parallel")),
        name="out_proj",
    )(o_da, o_ssm, o_fx, x, gate, w_out, g_post)


def _decode_kernel(pt_ref, *refs, mode, pages, past, lam_init):
    del pt_ref
    it = iter(refs)
    qm_ref = next(it)
    k_refs = [next(it) for _ in range(pages)]
    v_refs = [next(it) for _ in range(pages)]
    if mode == "fox":
        lf_refs = [next(it) for _ in range(pages)]
        small_ref = next(it)
    else:
        bias_ref = next(it)
        biasn_ref = next(it)
        lamp_ref = next(it)
    knew_ref = next(it)
    vnew_ref = next(it)
    gate_ref = next(it)
    nw_ref = next(it)
    o_ref = next(it)
    m_s = next(it)
    l_s = next(it)
    acc_s = next(it)
    carry_s = next(it) if mode == "fox" else None

    s_idx = pl.program_id(0)
    g = pl.program_id(1)
    ng = pl.num_programs(1)
    nrow = qm_ref.shape[0]
    span = pages * PAGE

    @pl.when(g == 0)
    def _():
        m_s[...] = jnp.full(m_s.shape, NEG_INF, F32)
        l_s[...] = jnp.zeros(l_s.shape, F32)
        acc_s[...] = jnp.zeros(acc_s.shape, F32)
        if mode == "fox":
            carry_s[...] = jnp.zeros(carry_s.shape, F32)

    qm = qm_ref[...]

    def update(s, v_list):
        m_prev = m_s[...]
        m_new = jnp.maximum(m_prev, jnp.max(s, axis=-1, keepdims=True))
        alpha = jnp.exp2(m_prev - m_new)
        p = jnp.exp2(s - m_new)
        l_s[...] = alpha * l_s[...] + jnp.sum(p, axis=-1, keepdims=True)
        pv = None
        for j, vt in enumerate(v_list):
            t = _dot_nt(p[:, j * PAGE:(j + 1) * PAGE].astype(BF16), vt)
            pv = t if pv is None else pv + t
        acc_s[...] = alpha * acc_s[...] + pv
        m_s[...] = m_new

    def expand_rows(ck):
        return jnp.concatenate(
            [jnp.broadcast_to(ck[h:h + 1, :], (SUBLANES, PAGE)) for h in range(N_HEADS)], axis=0)

    def upper(block):
        r = lax.broadcasted_iota(I32, (PAGE, PAGE), 0)
        c = lax.broadcasted_iota(I32, (PAGE, PAGE), 1)
        keep = r <= c
        if block:
            keep = keep & (r // block == c // block)
        return jnp.where(keep, 1.0, 0.0).astype(BF16)

    s_parts = [_dot(qm, k_refs[j][...].astype(BF16)) for j in range(pages)]
    if mode == "fox":
        lf = jnp.concatenate([lf_refs[j][...] for j in range(pages)], axis=0)
        cum = _dot_exact_rhs(lf, upper(0))
        carry = carry_s[...]
        for j in range(pages):
            cj = cum[j * SUBLANES:(j + 1) * SUBLANES]
            s_parts[j] = s_parts[j] - expand_rows((cj + carry) * LOG2E)
            carry = carry + cj[:, PAGE - 1:PAGE]
        carry_s[...] = carry
        s = jnp.concatenate(s_parts, axis=1)
    else:
        s = jnp.concatenate(s_parts, axis=1) + bias_ref[g]
    update(s, [v_refs[j][...].astype(BF16) for j in range(pages)])

    @pl.when(g == ng - 1)
    def _():
        sn = _dot(qm, knew_ref[...].astype(BF16))
        col = lax.broadcasted_iota(I32, (nrow, PAGE), 1)
        t_row = lax.broadcasted_iota(I32, (nrow, PAGE), 0) % SUBLANES
        rel = t_row - (col - s_idx * 4)
        ok = (rel >= 0) & (col >= s_idx * 4) & (t_row < 4)
        if mode == "fox":
            lfn = small_ref[...].T[SUBLANES:2 * SUBLANES, :]
            ckn = expand_rows((_dot_exact_rhs(lfn, upper(4)) + carry_s[...]) * LOG2E)
            cq = jnp.sum(jnp.where(rel == 0, ckn, 0.0), axis=-1, keepdims=True)
            m_s[...] = m_s[...] + cq
            sn = sn - ckn + cq
        else:
            bn = biasn_ref[...]
            add = jnp.zeros_like(sn)
            for r in range(4):
                add = jnp.where(rel == r, bn[:, r:r + 1], add)
            sn = sn + add
        sn = jnp.where(ok, sn, NEG_INF)
        update(sn, [vnew_ref[...].astype(BF16)])

        o1 = acc_s[...] / l_s[...]
        if mode == "da":
            a = o1[0:4 * SUBLANES] - _lam(lamp_ref, lam_init) * o1[4 * SUBLANES:]
        else:
            a = o1
        lane = lax.broadcasted_iota(I32, (SUBLANES, ATT_W), 1)
        o8 = jnp.zeros((SUBLANES, ATT_W), F32)
        for h in range(N_HEADS):
            o8 = jnp.where((lane >= h * HEAD_W) & (lane < (h + 1) * HEAD_W),
                           a[h * SUBLANES:(h + 1) * SUBLANES], o8)
        y = _head_norm(o8, N_HEADS) * nw_ref[...]
        if mode == "da":
            y = y * (1.0 - lam_init)
        o_ref[...] = y[0:4] * _silu(gate_ref[...])


def _decode(mode, layer, page_table, qm, k_cache, v_cache, knew, vnew, gate, norm_w, extra,
            pages, lam_init=0.0):
    nseq, nrow, _ = qm.shape
    npages = page_table.shape[1]
    ng = npages // pages
    past = npages * PAGE

    def page_spec(j, shape):
        nd = len(shape)
        return pl.BlockSpec((None, None) + shape,
                            lambda s, g, pt, j=j: (layer, pt[s * npages + g * pages + j]) + (0,) * nd)

    def const(shape):
        return pl.BlockSpec(shape, lambda s, g, pt: (0,) * len(shape))

    in_specs = [pl.BlockSpec((None, nrow, ATT_W), lambda s, g, pt: (s, 0, 0))]
    args = [qm]
    in_specs += [page_spec(j, (ATT_W, PAGE)) for j in range(pages)]
    args += [k_cache] * pages
    in_specs += [page_spec(j, (ATT_W, PAGE)) for j in range(pages)]
    args += [v_cache] * pages
    scratch = [pltpu.VMEM((nrow, 1), F32), pltpu.VMEM((nrow, 1), F32), pltpu.VMEM((nrow, ATT_W), F32)]
    if mode == "fox":
        lf_cache, small = extra
        in_specs += [page_spec(j, (SUBLANES, PAGE)) for j in range(pages)]
        args += [lf_cache] * pages
        in_specs.append(const((PAGE, LANES)))
        args.append(small)
        scratch.append(pltpu.VMEM((SUBLANES, PAGE), F32))
    else:
        bias, bias_new, lamp = extra
        in_specs += [const(bias.shape), const(bias_new.shape), const(lamp.shape)]
        args += [bias, bias_new, lamp]
    in_specs += [const((ATT_W, PAGE)), const((ATT_W, PAGE)),
                 pl.BlockSpec((None, 4, ATT_W), lambda s, g, pt: (s, 0, 0)),
                 const((1, ATT_W))]
    args += [knew, vnew, gate, norm_w]
    return pl.pallas_call(
        functools.partial(_decode_kernel, mode=mode, pages=pages, past=past, lam_init=lam_init),
        grid_spec=pltpu.PrefetchScalarGridSpec(
            num_scalar_prefetch=1, grid=(nseq, ng), in_specs=in_specs,
            out_specs=pl.BlockSpec((None, 4, ATT_W), lambda s, g, pt: (s, 0, 0)),
            scratch_shapes=scratch),
        out_shape=jax.ShapeDtypeStruct((nseq, 4, ATT_W), F32),
        compiler_params=_cp(("parallel", "arbitrary")),
        name="decode_" + mode,
    )(page_table.reshape(-1), *args)


def _pad_lanes(v, width=LANES):
    return jnp.pad(v, ((0, 0), (0, width - v.shape[-1])))


def _masked_queries(q, ncomp):
    nseq = q.shape[0]
    width = HEAD_W // ncomp
    lane = jnp.arange(ATT_W)
    q8 = jnp.pad(q, ((0, 0), (0, SUBLANES - 4), (0, 0)))
    blocks = []
    for c in range(ncomp):
        for h in range(N_HEADS):
            lo = h * HEAD_W + c * width
            blocks.append(jnp.where((lane >= lo) & (lane < lo + width), q8, jnp.zeros_like(q8)))
    return jnp.concatenate(blocks, axis=1).reshape(nseq, ncomp * N_HEADS * SUBLANES, ATT_W)


def kernel(x_prompt, x_sample, c_prompt, c_sample, cache_da_k, cache_da_v, cache_fox_k, cache_fox_v,
           cache_fox_logf, state_ssm, state_conv, page_table, rel_bias, w_ada, b_ada, g_pre, g_post,
           w_in, w_out, da_lam_q1, da_lam_k1, da_lam_q2, da_lam_k2, da_subln, conv_w, conv_b, dt_bias,
           a_log, d_skip, ssm_norm, fox_fbias, fox_norm):
    depth = w_in.shape[0]
    bsz, seq, _ = x_prompt.shape
    nseq, dec_len, _ = x_sample.shape
    nphys = cache_da_k.shape[1]
    npages = page_table.shape[1]
    past = npages * PAGE
    assert dec_len == 4 and nseq * dec_len == PAGE
    tile = 512 if seq % 512 == 0 else 128
    chunk = 128
    pages = 16 if npages % 16 == 0 else npages

    def cols(a, b):
        return w_in[:, :, a:b]

    w_rows = jnp.concatenate([cols(C_DAK, C_DAV), cols(C_DAG, C_Z), cols(C_Z, C_XBC), cols(C_XBC, C_DT),
                              cols(C_FXK, C_FXV), cols(C_FXG, C_FXF)], axis=-1).astype(BF16)
    w_kvt = jnp.swapaxes(jnp.concatenate([cols(C_DAQ, C_DAG), cols(C_FXQ, C_FXG)], axis=-1), 1, 2).astype(BF16)
    w_small = jnp.pad(jnp.concatenate([cols(C_DT, C_FXQ), cols(C_FXF, C_FXF + 4)], axis=-1),
                      ((0, 0), (0, 0), (0, LANES - 12))).astype(BF16)
    small_bias = _pad_lanes(jnp.concatenate([dt_bias, fox_fbias], axis=-1))
    w_out_b = w_out.astype(BF16)
    lamp = jnp.pad(jnp.stack([da_lam_q1, da_lam_k1, da_lam_q2, da_lam_k2], axis=1),
                   ((0, 0), (0, SUBLANES - 4), (0, LANES - DA_QK)))
    conv_w8 = jnp.pad(conv_w, ((0, 0), (0, SUBLANES - CONV_W), (0, 0)))
    a_log_p = _pad_lanes(a_log)
    d_skip_e = jnp.repeat(d_skip, HEAD_W, axis=-1)
    da_nw2 = jnp.tile(da_subln, (1, 2))
    da_nw4 = jnp.tile(da_subln, (1, 4))
    fx_nw2 = jnp.tile(fox_norm, (1, 2))
    fx_nw4 = jnp.tile(fox_norm, (1, 4))

    def pages_t(c):
        return jnp.transpose(c, (0, 1, 3, 4, 2)).reshape(depth, nphys, ATT_W, PAGE)

    kc_da, vc_da, kc_fx, vc_fx = (pages_t(c) for c in (cache_da_k, cache_da_v, cache_fox_k, cache_fox_v))
    lf_cache = jnp.pad(jnp.transpose(cache_fox_logf, (0, 1, 3, 2)),
                       ((0, 0), (0, 0), (0, SUBLANES - N_HEADS), (0, 0)))
    conv_init_s = jnp.pad(state_conv, ((0, 0), (0, 0), (SUBLANES - 3, 0), (0, 0)))
    ssm_init_s = state_ssm.reshape(depth, nseq, 4, LANES, SSM_STATE)
    conv_init_p = jnp.zeros((bsz, SUBLANES, CONV_CH), F32)
    ssm_init_p = jnp.zeros((bsz, 4, LANES, SSM_STATE), F32)

    nc = bsz + nseq
    nc_pad = -(-nc // SUBLANES) * SUBLANES
    c_all = jnp.pad(jnp.concatenate([c_prompt, c_sample], axis=0), ((0, nc_pad - nc), (0, 0)))
    mod = _adaln(c_all, w_ada, b_ada)

    ii = jnp.arange(tile, dtype=I32)
    d = ii[None, :] - ii[:, None]
    rel_tiles = jnp.concatenate([d + 2 * tile, d + tile, d], axis=0)
    bias_tiles = _t5_bias(rel_bias, rel_tiles, tile).reshape(N_HEADS, 3, tile, tile) * LOG2E
    tt = jnp.arange(SUBLANES, dtype=I32)
    rel_dec = jnp.where(tt[:, None] < dec_len, past + tt[:, None] - jnp.arange(past, dtype=I32)[None, :], 1)
    bias_dec = _t5_bias(rel_bias, rel_dec, SUBLANES)
    bias_dec = jnp.tile(bias_dec.reshape(N_HEADS * SUBLANES, past), (2, 1))
    bias_dec = jnp.transpose(bias_dec.reshape(2 * N_HEADS * SUBLANES, npages // pages, pages * PAGE),
                             (1, 0, 2)) * LOG2E
    bias_new = jnp.tile(jnp.repeat(_pad_lanes(rel_bias[:4].T), SUBLANES, axis=0), (2, 1)) * LOG2E
    fox_mask = jnp.stack([jnp.zeros((tile, tile), F32), jnp.zeros((tile, tile), F32),
                          jnp.where(d >= 0, 0.0, NEG_INF).astype(F32)])[None]

    xp = x_prompt
    xs = x_sample.reshape(1, nseq * dec_len, D_MODEL)
    outs_p = [[] for _ in range(7)]
    outs_s = [[] for _ in range(7)]
    for l in range(depth):
        lam_init = 0.8 - 0.6 * math.exp(-0.3 * l)
        shift_p, scale_p, gate_p = (mod[l, :bsz, i * D_MODEL:(i + 1) * D_MODEL][:, None] for i in range(3))
        shift_s, scale_s, gate_s = (
            jnp.repeat(mod[l, bsz:nc, i * D_MODEL:(i + 1) * D_MODEL], dec_len, axis=0)[None] for i in range(3))
        gp = g_pre[l][None]
        sb = small_bias[l][None]

        (kdar, gda, z, xbc, small, kfxr, gfx, qdat, kda, vda, vdab, qfxt, kfx, vfx, vfxb) = _in_proj(
            xp, shift_p, scale_p, gp, w_rows[l], w_kvt[l], w_small[l], sb, tile)
        ckrep, cqrow = _cumsum(small, tile)
        o_da = _flash("da", qdat, kdar, vdab, gda, da_nw2[l][None], bias_tiles, (lamp[l],), tile, lam_init)
        o_fx = _flash("fox", qfxt, kfxr, vfxb, gfx, fx_nw2[l][None], fox_mask, (ckrep, cqrow), tile)
        o_ssm, ssm_fin = _ssd(xbc, z, small, conv_init_p, ssm_init_p, conv_w8[l], conv_b[l][None],
                              a_log_p[l][None], d_skip_e[l][None], ssm_norm[l][None], chunk, None)
        xp = _out_proj(o_da, o_ssm, o_fx, xp, gate_p, w_out_b[l], g_post[l][None], tile)

        def heads_t(a, n=bsz, s=seq):
            return jnp.transpose(a.reshape(n, N_HEADS, HEAD_W, s), (0, 3, 1, 2))

        for lst, val in zip(outs_p, (heads_t(kda), heads_t(vda), heads_t(kfx), heads_t(vfx),
                                     small[:, :, 8:12], ssm_fin.reshape(bsz, SSM_HEADS, HEAD_W, SSM_STATE),
                                     xbc[:, seq - 3:, :])):
            lst.append(val)

        (_, gda, z, xbc, small, _, gfx, qdat, kda, vda, _, qfxt, kfx, vfx, _) = _in_proj(
            xs, shift_s, scale_s, gp, w_rows[l], w_kvt[l], w_small[l], sb, PAGE)
        qm_da = _masked_queries(qdat[0].T.reshape(nseq, dec_len, ATT_W), 2)
        qm_fx = _masked_queries(qfxt[0].T.reshape(nseq, dec_len, ATT_W), 1)
        o_da = _decode("da", l, page_table, qm_da, kc_da, vc_da, kda[0], vda[0],
                       gda.reshape(nseq, dec_len, ATT_W), da_nw4[l][None],
                       (bias_dec, bias_new, lamp[l]), pages, lam_init)
        o_fx = _decode("fox", l, page_table, qm_fx, kc_fx, vc_fx, kfx[0], vfx[0],
                       gfx.reshape(nseq, dec_len, ATT_W), fx_nw4[l][None],
                       (lf_cache, small[0]), pages)

        def pad_seq(a):
            a = a.reshape(nseq, dec_len, a.shape[-1])
            return jnp.pad(a, ((0, 0), (0, chunk - dec_len), (0, 0)))

        o_ssm, ssm_fin = _ssd(pad_seq(xbc), pad_seq(z), pad_seq(small), conv_init_s[l], ssm_init_s[l],
                              conv_w8[l], conv_b[l][None], a_log_p[l][None], d_skip_e[l][None],
                              ssm_norm[l][None], chunk, dec_len)
        o_ssm = o_ssm[:, :dec_len].reshape(1, nseq * dec_len, SSM_W)
        xs = _out_proj(o_da.reshape(1, nseq * dec_len, ATT_W), o_ssm, o_fx.reshape(1, nseq * dec_len, ATT_W),
                       xs, gate_s, w_out_b[l], g_post[l][None], PAGE)

        def heads_s(a):
            return heads_t(a, 1, nseq * dec_len).reshape(nseq, dec_len, N_HEADS, HEAD_W)

        xbc_s = xbc.reshape(nseq, dec_len, CONV_CH)
        for lst, val in zip(outs_s, (heads_s(kda), heads_s(vda), heads_s(kfx), heads_s(vfx),
                                     small.reshape(nseq, dec_len, LANES)[:, :, 8:12],
                                     ssm_fin.reshape(nseq, SSM_HEADS, HEAD_W, SSM_STATE),
                                     xbc_s[:, dec_len - 3:, :])):
            lst.append(val)

    y_s = xs.reshape(nseq, dec_len, D_MODEL)
    return (xp, y_s) + tuple(jnp.stack(v) for v in outs_p) + tuple(jnp.stack(v) for v in outs_s)
```

```python
import functools
import math

import jax
import jax.numpy as jnp
from jax import lax
from jax.experimental import pallas as pl
from jax.experimental.pallas import tpu as pltpu

F32 = jnp.float32
BF16 = jnp.bfloat16
I32 = jnp.int32

EPS = 1e-6
NEG_INF = -1e30

D_MODEL = 1024
N_HEADS = 4
HEAD_W = 64
ATT_W = N_HEADS * HEAD_W
DA_QK = 32
SSM_HEADS = 8
SSM_W = 512
SSM_STATE = 128
CONV_W = 4
CONV_CH = 1024
N_BUCKETS = 32
MAX_DISTANCE = 128
PAGE = 128
LANES = 128
SUBLANES = 8
VMEM_LIMIT = 56 * 1024 * 1024

DA_SCALE = DA_QK ** -0.5
FOX_SCALE = HEAD_W ** -0.5
LOG2E = math.log2(math.e)

_SIZES = (256, 256, 256, 256, 512, 1024, 8, 256, 256, 256, 256, 4)
_OFFS = [0]
for _s in _SIZES:
    _OFFS.append(_OFFS[-1] + _s)
(C_DAQ, C_DAK, C_DAV, C_DAG, C_Z, C_XBC, C_DT, C_FXQ, C_FXK, C_FXV, C_FXG, C_FXF) = _OFFS[:12]


def _cp(sem):
    return pltpu.CompilerParams(dimension_semantics=sem, vmem_limit_bytes=VMEM_LIMIT)


def _silu(v):
    return v * (1.0 / (1.0 + jnp.exp(-v)))


def _split3(v):
    a = v.astype(BF16)
    r = v - a.astype(F32)
    b = r.astype(BF16)
    r = r - b.astype(F32)
    return a, b, r.astype(BF16)


def _dot(a, b):
    return jnp.dot(a, b, preferred_element_type=F32)


def _dot_nt(a, b):
    return lax.dot_general(a, b, (((1,), (1,)), ((), ())), preferred_element_type=F32)


def _dot_exact_rhs(v, w):
    a, b, c = _split3(v)
    return _dot(a, w) + _dot(b, w) + _dot(c, w)


def _dot_exact_lhs(w, v):
    a, b, c = _split3(v)
    return _dot(w, a) + _dot(w, b) + _dot(w, c)


def _adaln_kernel(c_ref, w_ref, b_ref, o_ref):
    c = c_ref[...]
    o_ref[...] = _dot(_silu(c), w_ref[...]) + b_ref[...]


def _adaln(c_all, w_ada, b_ada):
    depth = w_ada.shape[0]
    rows = c_all.shape[0]
    nblk = 3
    return pl.pallas_call(
        _adaln_kernel,
        grid=(depth, nblk),
        in_specs=[
            pl.BlockSpec((rows, D_MODEL), lambda l, j: (0, 0)),
            pl.BlockSpec((None, D_MODEL, D_MODEL), lambda l, j: (l, 0, j)),
            pl.BlockSpec((None, 1, D_MODEL), lambda l, j: (l, 0, j)),
        ],
        out_specs=pl.BlockSpec((None, rows, D_MODEL), lambda l, j: (l, 0, j)),
        out_shape=jax.ShapeDtypeStruct((depth, rows, 3 * D_MODEL), F32),
        compiler_params=_cp(("parallel", "parallel")),
        name="adaln",
    )(c_all, w_ada, b_ada.reshape(depth, 1, 3 * D_MODEL))


def _t5_kernel(tab_ref, rel_ref, o_ref):
    rel = rel_ref[...]
    n = jnp.maximum(rel, 0)
    max_exact = N_BUCKETS // 2
    nf = jnp.maximum(n, 1).astype(F32)
    large = max_exact + (jnp.log(nf / max_exact) / math.log(MAX_DISTANCE / max_exact)
                         * (N_BUCKETS - max_exact)).astype(I32)
    large = jnp.minimum(large, N_BUCKETS - 1)
    bucket = jnp.where(n < max_exact, n, large)
    for h in range(N_HEADS):
        acc = jnp.zeros(rel.shape, F32)
        for k in range(N_BUCKETS):
            acc = jnp.where(bucket == k, tab_ref[k, h], acc)
        o_ref[h] = jnp.where(rel >= 0, acc, NEG_INF)


def _t5_bias(rel_bias, rel, block_rows):
    rows, cols = rel.shape
    return pl.pallas_call(
        _t5_kernel,
        grid=(rows // block_rows,),
        in_specs=[
            pl.BlockSpec(memory_space=pltpu.SMEM),
            pl.BlockSpec((block_rows, cols), lambda i: (i, 0)),
        ],
        out_specs=pl.BlockSpec((N_HEADS, block_rows, cols), lambda i: (0, i, 0)),
        out_shape=jax.ShapeDtypeStruct((N_HEADS, rows, cols), F32),
        compiler_params=_cp(("parallel",)),
        name="t5_bias",
    )(rel_bias, rel)


def _in_proj_kernel(x_ref, shift_ref, scale_ref, g_ref, wr_ref, wt_ref, ws_ref, sb_ref,
                    kda_ref, gda_ref, z_ref, xbc_ref, small_ref, kfx_ref, gfx_ref,
                    qdat_ref, kdat_ref, vdat_ref, vdatb_ref, qfxt_ref, kfxt_ref, vfxt_ref, vfxtb_ref):
    x = x_ref[...]
    y = x * lax.rsqrt(jnp.mean(x * x, axis=-1, keepdims=True) + EPS)
    h = (y * g_ref[...]) * (1.0 + scale_ref[...]) + shift_ref[...]
    hb = h.astype(BF16)

    def rows(a, b):
        return _dot(hb, wr_ref[:, a:b])

    kda_ref[...] = rows(0, 256).astype(BF16)
    gda_ref[...] = rows(256, 512)
    z_ref[...] = rows(512, 1024)
    xbc_ref[...] = rows(1024, 2048)
    kfx_ref[...] = rows(2048, 2304).astype(BF16)
    gfx_ref[...] = rows(2304, 2560)

    v = _dot(hb, ws_ref[...]) + sb_ref[...]
    t = jnp.log1p(jnp.exp(-jnp.abs(v)))
    lane = lax.broadcasted_iota(I32, v.shape, 1)
    small_ref[...] = jnp.where(lane < 8, jnp.maximum(v, 0.0) + t,
                               jnp.where(lane < 12, jnp.minimum(v, 0.0) - t, 0.0))

    def cols(j):
        return _dot_nt(wt_ref[j * ATT_W:(j + 1) * ATT_W, :], hb)

    qdat_ref[...] = (cols(0) * (DA_SCALE * LOG2E)).astype(BF16)
    kdat_ref[...] = cols(1)
    v = cols(2)
    vdat_ref[...] = v
    vdatb_ref[...] = v.astype(BF16)
    qfxt_ref[...] = (cols(3) * (FOX_SCALE * LOG2E)).astype(BF16)
    kfxt_ref[...] = cols(4)
    v = cols(5)
    vfxt_ref[...] = v
    vfxtb_ref[...] = v.astype(BF16)


def _in_proj(x, shift, scale, g_pre, w_rows, w_kvt, w_small, small_bias, tm):
    bsz, seq, _ = x.shape
    nt = seq // tm
    mod_rows = shift.shape[1]
    if mod_rows == 1:
        mod_spec = pl.BlockSpec((None, 1, D_MODEL), lambda b, i: (b, 0, 0))
    else:
        mod_spec = pl.BlockSpec((None, tm, D_MODEL), lambda b, i: (b, i, 0))

    def const(shape):
        return pl.BlockSpec(shape, lambda b, i: (0,) * len(shape))

    def row_spec(width):
        return pl.BlockSpec((None, tm, width), lambda b, i: (b, i, 0))

    t_spec = pl.BlockSpec((None, ATT_W, tm), lambda b, i: (b, 0, i))
    tb_spec = pl.BlockSpec((None, None, ATT_W, tm), lambda b, i: (b, i, 0, 0))

    def row_shape(width, dt=F32):
        return jax.ShapeDtypeStruct((bsz, seq, width), dt)

    t_shape = jax.ShapeDtypeStruct((bsz, ATT_W, seq), F32)
    tq_shape = jax.ShapeDtypeStruct((bsz, ATT_W, seq), BF16)
    tb_shape = jax.ShapeDtypeStruct((bsz, nt, ATT_W, tm), BF16)
    return pl.pallas_call(
        _in_proj_kernel,
        grid=(bsz, nt),
        in_specs=[row_spec(D_MODEL), mod_spec, mod_spec, const((1, D_MODEL)),
                  const(w_rows.shape), const(w_kvt.shape), const(w_small.shape), const((1, LANES))],
        out_specs=[row_spec(256), row_spec(256), row_spec(512), row_spec(1024), row_spec(LANES),
                   row_spec(256), row_spec(256),
                   t_spec, t_spec, t_spec, tb_spec, t_spec, t_spec, t_spec, tb_spec],
        out_shape=[row_shape(256, BF16), row_shape(256), row_shape(512), row_shape(1024),
                   row_shape(LANES), row_shape(256, BF16), row_shape(256),
                   tq_shape, t_shape, t_shape, tb_shape, tq_shape, t_shape, t_shape, tb_shape],
        compiler_params=_cp(("parallel", "parallel")),
        name="in_proj",
    )(x, shift, scale, g_pre, w_rows, w_kvt, w_small, small_bias)


def _cumsum_kernel(x_ref, rep_ref, row_ref, carry_s):
    @pl.when(pl.program_id(1) == 0)
    def _():
        carry_s[...] = jnp.zeros_like(carry_s)

    x = x_ref[...]
    t = x.shape[0]
    r = lax.broadcasted_iota(I32, (t, t), 0)
    c = lax.broadcasted_iota(I32, (t, t), 1)
    tri = jnp.where(r >= c, 1.0, 0.0).astype(BF16)
    cum = _dot_exact_lhs(tri, x) + carry_s[...]
    carry_s[...] = cum[t - 1:t, :]
    sr = lax.broadcasted_iota(I32, (LANES, N_HEADS * LANES), 0)
    sc = lax.broadcasted_iota(I32, (LANES, N_HEADS * LANES), 1)
    spread = jnp.where(sr == SUBLANES + sc // LANES, 1.0, 0.0).astype(BF16)
    rep_ref[...] = _dot_exact_rhs(cum, spread) * LOG2E
    row_ref[...] = cum.T[SUBLANES:2 * SUBLANES, :] * LOG2E


def _cumsum(small, tm):
    bsz, seq, _ = small.shape
    return pl.pallas_call(
        _cumsum_kernel,
        grid=(bsz, seq // tm),
        in_specs=[pl.BlockSpec((None, tm, LANES), lambda b, i: (b, i, 0))],
        out_specs=[pl.BlockSpec((None, tm, N_HEADS * LANES), lambda b, i: (b, i, 0)),
                   pl.BlockSpec((None, None, SUBLANES, tm), lambda b, i: (b, i, 0, 0))],
        out_shape=[jax.ShapeDtypeStruct((bsz, seq, N_HEADS * LANES), F32),
                   jax.ShapeDtypeStruct((bsz, seq // tm, SUBLANES, tm), F32)],
        scratch_shapes=[pltpu.VMEM((1, LANES), F32)],
        compiler_params=_cp(("parallel", "arbitrary")),
        name="logf_cumsum",
    )(small)


def _lam(lamp_ref, lam_init):
    p = lamp_ref[...]
    s1 = jnp.sum(p[0:1] * p[1:2], axis=-1, keepdims=True)
    s2 = jnp.sum(p[2:3] * p[3:4], axis=-1, keepdims=True)
    return jnp.exp(s1) - jnp.exp(s2) + lam_init


def _head_norm(o, n_heads):
    lane = lax.broadcasted_iota(I32, o.shape, 1)
    o2 = o * o
    rs = jnp.zeros_like(o)
    for h in range(n_heads):
        m = (lane >= h * HEAD_W) & (lane < (h + 1) * HEAD_W)
        ss = jnp.sum(jnp.where(m, o2, 0.0), axis=-1, keepdims=True)
        rs = jnp.where(m, lax.rsqrt(ss / HEAD_W + EPS), rs)
    return o * rs


def _flash_kernel(*refs, mode, tile, lam_init):
    if mode == "da":
        (qt_ref, k_ref, v_ref, gate_ref, nw_ref, bias_ref, lamp_ref,
         o_ref, qm_s, m_s, mb_s, acc_s, s_s) = refs
        ncomp = 2
    else:
        (qt_ref, k_ref, v_ref, gate_ref, nw_ref, bias_ref, ckrep_ref, cqrow_ref,
         o_ref, qm_s, m_s, mb_s, acc_s, s_s, cq_s) = refs
        ncomp = 1
    hp = pl.program_id(1)
    qi = pl.program_id(2)
    nr = 2 * ncomp
    width = HEAD_W // ncomp
    row = lax.broadcasted_iota(I32, (LANES, tile), 0)

    qt = qt_ref[...]
    for r in range(nr):
        sel = (row >= r * width) & (row < (r + 1) * width)
        qm_s[r] = jnp.where(sel, qt, jnp.zeros_like(qt))
        m_s[r] = jnp.full((1, tile), NEG_INF, F32)
        acc_s[r] = jnp.zeros((LANES, tile), F32)
    if mode == "fox":
        for hh in range(2):
            cq_s[hh] = cqrow_ref[pl.ds(hp * 2 + hh, 1), :]

    def stage1(kj, slot):
        ks = pl.multiple_of(kj * tile, tile)
        kt = k_ref[pl.ds(ks, tile), :]
        kind = jnp.clip(kj - qi + 2, 0, 2)
        for r in range(nr):
            hh = r // ncomp
            s = _dot(kt, qm_s[r])
            if mode == "da":
                s = s + bias_ref[hh, kind]
                shift = 0.0
            else:
                ck = ckrep_ref[pl.ds(ks, tile), hh * LANES:(hh + 1) * LANES]
                s = s + bias_ref[0, kind] - jnp.tile(ck, (1, tile // LANES))
                shift = cq_s[hh]
            m_prev = m_s[r]
            m_new = jnp.maximum(m_prev, jnp.max(s, axis=0, keepdims=True) + shift)
            s_s[slot, r] = s
            mb_s[r] = m_prev
            m_s[r] = m_new

    def stage2_inputs():
        return [(mb_s[r], m_s[r]) for r in range(nr)]

    def stage2(kj, slot, pend):
        vt = v_ref[kj]
        for r in range(nr):
            hh = r // ncomp
            m_prev, m_new = pend[r]
            shift = cq_s[hh] if mode == "fox" else 0.0
            alpha = jnp.exp2(m_prev - m_new)
            p = jnp.exp2(s_s[slot, r] - (m_new - shift)).astype(BF16)
            own = (row >= hh * HEAD_W) & (row < (hh + 1) * HEAD_W)
            vmod = jnp.where(own, vt, jnp.ones_like(vt))
            acc_s[r] = alpha * acc_s[r] + _dot(vmod, p)

    def body(kj, carry):
        for par in range(2):
            @pl.when(kj % 2 == par)
            def _():
                pend = stage2_inputs()
                stage1(kj, par)
                stage2(kj - 1, 1 - par, pend)
        return carry

    stage1(0, 0)
    lax.fori_loop(1, qi + 1, body, 0)
    for par in range(2):
        @pl.when(qi % 2 == par)
        def _():
            stage2(qi, par, stage2_inputs())

    outs = []
    for hh in range(2):
        den = HEAD_W * (1 - hh)
        if mode == "da":
            a0 = acc_s[2 * hh]
            a1 = acc_s[2 * hh + 1]
            outs.append(a0 / a0[den:den + 1, :] - _lam(lamp_ref, lam_init) * (a1 / a1[den:den + 1, :]))
        else:
            a0 = acc_s[hh]
            outs.append(a0 / a0[den:den + 1, :])
    o = jnp.where(row < HEAD_W, outs[0], outs[1]).T
    y = _head_norm(o, 2) * nw_ref[...]
    if mode == "da":
        y = y * (1.0 - lam_init)
    o_ref[...] = y * _silu(gate_ref[...])


def _flash(mode, qt, k, vt, gate, norm_w, bias, extra, tile, lam_init=0.0):
    bsz, seq, _ = k.shape
    nq = seq // tile
    ncomp = 2 if mode == "da" else 1
    nr = 2 * ncomp
    blk = pl.BlockSpec((None, tile, LANES), lambda b, p, i: (b, i, p))
    qt_spec = pl.BlockSpec((None, LANES, tile), lambda b, p, i: (b, p, i))
    k_spec = pl.BlockSpec((None, seq, LANES), lambda b, p, i: (b, 0, p))
    v_spec = pl.BlockSpec((None, nq, LANES, tile), lambda b, p, i: (b, 0, p, 0))
    nw_spec = pl.BlockSpec((1, LANES), lambda b, p, i: (0, 0))
    in_specs = [qt_spec, k_spec, v_spec, blk, nw_spec]
    scratch = [pltpu.VMEM((nr, LANES, tile), BF16), pltpu.VMEM((nr, 1, tile), F32),
               pltpu.VMEM((nr, 1, tile), F32), pltpu.VMEM((nr, LANES, tile), F32),
               pltpu.VMEM((2, nr, tile, tile), F32)]
    if mode == "da":
        in_specs += [pl.BlockSpec((2, 3, tile, tile), lambda b, p, i: (p, 0, 0, 0)),
                     pl.BlockSpec((SUBLANES, LANES), lambda b, p, i: (0, 0))]
    else:
        in_specs += [pl.BlockSpec((1, 3, tile, tile), lambda b, p, i: (0, 0, 0, 0)),
                     pl.BlockSpec((None, seq, 2 * LANES), lambda b, p, i: (b, 0, p)),
                     pl.BlockSpec((None, None, SUBLANES, tile), lambda b, p, i: (b, i, 0, 0))]
        scratch.append(pltpu.VMEM((2, 1, tile), F32))
    return pl.pallas_call(
        functools.partial(_flash_kernel, mode=mode, tile=tile, lam_init=lam_init),
        grid=(bsz, 2, nq),
        in_specs=in_specs,
        out_specs=blk,
        out_shape=jax.ShapeDtypeStruct((bsz, seq, ATT_W), F32),
        scratch_shapes=scratch,
        compiler_params=_cp(("parallel", "parallel", "arbitrary")),
        name="flash_" + mode,
    )(qt, k, vt, gate, norm_w, bias, *extra)


def _ssd_kernel(xbc_ref, z_ref, small_ref, cinit_ref, sinit_ref, cw_ref, cb_ref, alog_ref,
                dskip_ref, nw_ref, o_ref, sfin_ref, state_s, tail_s, *, chunk, valid_len):
    c = pl.program_id(1)
    nchunks = pl.num_programs(1)

    @pl.when(c == 0)
    def _():
        state_s[...] = sinit_ref[...]
        tail_s[...] = cinit_ref[...]

    x = xbc_ref[...]
    xcat = jnp.concatenate([tail_s[...], x], axis=0)
    conv = cb_ref[...] + xcat[SUBLANES - 3:SUBLANES - 3 + chunk] * cw_ref[0:1, :]
    for j in range(1, CONV_W):
        conv = conv + xcat[SUBLANES - 3 + j:SUBLANES - 3 + j + chunk] * cw_ref[j:j + 1, :]
    tail_s[...] = x[chunk - SUBLANES:chunk]
    xc = _silu(conv)
    xs = xc[:, :SSM_W]
    bm = xc[:, SSM_W:SSM_W + 2 * SSM_STATE].astype(BF16)
    cm = xc[:, SSM_W + 2 * SSM_STATE:].astype(BF16)

    lane = lax.broadcasted_iota(I32, (chunk, LANES), 1)
    row = lax.broadcasted_iota(I32, (chunk, LANES), 0)
    dtv = jnp.where(lane < SSM_HEADS, small_ref[...], 0.0)
    if valid_len is not None:
        dtv = jnp.where(row + c * chunk < valid_len, dtv, 0.0)
    d_a = dtv * (-jnp.exp(alog_ref[...]))

    ri = lax.broadcasted_iota(I32, (chunk, chunk), 0)
    ci = lax.broadcasted_iota(I32, (chunk, chunk), 1)
    causal = ri >= ci
    tri = jnp.where(causal, 1.0, 0.0).astype(BF16)
    a_cs = _dot_exact_lhs(tri, d_a)
    a_cs_t = a_cs.T

    er = lax.broadcasted_iota(I32, (LANES, SSM_W), 0)
    ec = lax.broadcasted_iota(I32, (LANES, SSM_W), 1)
    expand = jnp.where((ec >= er * HEAD_W) & (ec < (er + 1) * HEAD_W), 1.0, 0.0).astype(BF16)
    a_exp = _dot_exact_rhs(a_cs, expand)
    dt_exp = _dot_exact_rhs(dtv, expand)
    tot_exp = a_exp[chunk - 1:chunk, :]
    e_col = jnp.exp(a_exp)
    xdt = xs * dt_exp
    xw = xdt * jnp.exp(tot_exp - a_exp)

    srow = lax.broadcasted_iota(I32, (LANES, 1), 0)
    ys = []
    for g in range(2):
        bg = bm[:, g * SSM_STATE:(g + 1) * SSM_STATE]
        cg = cm[:, g * SSM_STATE:(g + 1) * SSM_STATE]
        scores = _dot_nt(cg, bg)
        for pp in range(2):
            p = g * 2 + pp
            sl = slice(p * LANES, (p + 1) * LANES)
            xdt_p = xdt[:, sl].astype(BF16)
            yd = []
            for h in (2 * p, 2 * p + 1):
                seg = a_cs[:, h:h + 1] - a_cs_t[h:h + 1, :]
                decay = jnp.exp(jnp.where(causal, seg, NEG_INF))
                yd.append(_dot((scores * decay).astype(BF16), xdt_p))
            y_diag = jnp.where(lane < HEAD_W, yd[0], yd[1])
            prev = state_s[p]
            y_off = _dot_nt(cg, prev.astype(BF16)) * e_col[:, sl]
            cs = _dot(xw[:, sl].T.astype(BF16), bg)
            e_tot = jnp.where(srow < HEAD_W, jnp.exp(a_cs[chunk - 1:chunk, 2 * p:2 * p + 1]),
                              jnp.exp(a_cs[chunk - 1:chunk, 2 * p + 1:2 * p + 2]))
            state_s[p] = prev * e_tot + cs
            ys.append(y_diag + y_off + dskip_ref[:, sl] * xs[:, sl])
    y = jnp.concatenate(ys, axis=1)
    yz = y * _silu(z_ref[...])
    o_ref[...] = yz * lax.rsqrt(jnp.mean(yz * yz, axis=-1, keepdims=True) + EPS) * nw_ref[...]

    @pl.when(c == nchunks - 1)
    def _():
        sfin_ref[...] = state_s[...]


def _ssd(xbc, z, small, conv_init, ssm_init, conv_w, conv_b, a_log, d_skip, norm_w, chunk, valid_len):
    bsz, seq, _ = xbc.shape

    def row_spec(width):
        return pl.BlockSpec((None, chunk, width), lambda b, c: (b, c, 0))

    def const(shape):
        return pl.BlockSpec(shape, lambda b, c: (0,) * len(shape))

    st_spec = pl.BlockSpec((None, 4, LANES, SSM_STATE), lambda b, c: (b, 0, 0, 0))
    return pl.pallas_call(
        functools.partial(_ssd_kernel, chunk=chunk, valid_len=valid_len),
        grid=(bsz, seq // chunk),
        in_specs=[row_spec(CONV_CH), row_spec(SSM_W), row_spec(LANES),
                  pl.BlockSpec((None, SUBLANES, CONV_CH), lambda b, c: (b, 0, 0)), st_spec,
                  const((SUBLANES, CONV_CH)), const((1, CONV_CH)), const((1, LANES)),
                  const((1, SSM_W)), const((1, SSM_W))],
        out_specs=[row_spec(SSM_W), st_spec],
        out_shape=[jax.ShapeDtypeStruct((bsz, seq, SSM_W), F32),
                   jax.ShapeDtypeStruct((bsz, 4, LANES, SSM_STATE), F32)],
        scratch_shapes=[pltpu.VMEM((4, LANES, SSM_STATE), F32), pltpu.VMEM((SUBLANES, CONV_CH), F32)],
        compiler_params=_cp(("parallel", "arbitrary")),
        name="ssd",
    )(xbc, z, small, conv_init, ssm_init, conv_w, conv_b, a_log, d_skip, norm_w)


def _out_proj_kernel(oda_ref, ossm_ref, ofx_ref, x_ref, gate_ref, w_ref, g_ref, y_ref):
    mix = (_dot(oda_ref[...].astype(BF16), w_ref[0:ATT_W, :])
           + _dot(ossm_ref[...].astype(BF16), w_ref[ATT_W:ATT_W + SSM_W, :])
           + _dot(ofx_ref[...].astype(BF16), w_ref[ATT_W + SSM_W:, :]))
    n = mix * lax.rsqrt(jnp.mean(mix * mix, axis=-1, keepdims=True) + EPS) * g_ref[...]
    y_ref[...] = x_ref[...] + gate_ref[...] * n


def _out_proj(o_da, o_ssm, o_fx, x, gate, w_out, g_post, tm):
    bsz, seq, _ = x.shape
    if gate.shape[1] == 1:
        gate_spec = pl.BlockSpec((None, 1, D_MODEL), lambda b, i: (b, 0, 0))
    else:
        gate_spec = pl.BlockSpec((None, tm, D_MODEL), lambda b, i: (b, i, 0))

    def row_spec(width):
        return pl.BlockSpec((None, tm, width), lambda b, i: (b, i, 0))

    return pl.pallas_call(
        _out_proj_kernel,
        grid=(bsz, seq // tm),
        in_specs=[row_spec(ATT_W), row_spec(SSM_W), row_spec(ATT_W), row_spec(D_MODEL), gate_spec,
                  pl.BlockSpec((D_MODEL, D_MODEL), lambda b, i: (0, 0)),
                  pl.BlockSpec((1, D_MODEL), lambda b, i: (0, 0))],
        out_specs=row_spec(D_MODEL),
        out_shape=jax.ShapeDtypeStruct((bsz, seq, D_MODEL), F32),
        compiler_params=_cp(("parallel", "parallel")),
        name="out_proj",
    )(o_da, o_ssm, o_fx, x, gate, w_out, g_post)


def _decode_kernel(pt_ref, *refs, mode, layer, pages, ng, lam_init):
    it = iter(refs)
    qm_ref = next(it)
    caches = [next(it), next(it)]
    if mode == "fox":
        caches.append(next(it))
        small_ref = next(it)
    else:
        bias_ref = next(it)
        biasn_ref = next(it)
        lamp_ref = next(it)
    knew_ref = next(it)
    vnew_ref = next(it)
    gate_ref = next(it)
    nw_ref = next(it)
    o_ref = next(it)
    bufs = [next(it) for _ in caches]
    sem = next(it)
    m_s = next(it)
    l_s = next(it)
    acc_s = next(it)
    carry_s = next(it) if mode == "fox" else None

    step = pl.program_id(0)
    nsteps = pl.num_programs(0)
    s_idx = step // ng
    g = step % ng
    slot = step % 2
    nrow = qm_ref.shape[0]

    def page_copy(kind, t, sl, j):
        page = pt_ref[t * pages + j]
        return pltpu.make_async_copy(caches[kind].at[layer, page], bufs[kind].at[sl, j], sem.at[kind, sl])

    def start_fetch(t, sl):
        for kind in range(len(caches)):
            for j in range(pages):
                page_copy(kind, t, sl, j).start()

    @pl.when(step == 0)
    def _():
        start_fetch(step, slot)

    @pl.when(step + 1 < nsteps)
    def _():
        start_fetch(step + 1, 1 - slot)

    for kind in range(len(caches)):
        for j in range(pages):
            page_copy(kind, step, slot, j).wait()

    @pl.when(g == 0)
    def _():
        m_s[...] = jnp.full(m_s.shape, NEG_INF, F32)
        l_s[...] = jnp.zeros(l_s.shape, F32)
        acc_s[...] = jnp.zeros(acc_s.shape, F32)
        if mode == "fox":
            carry_s[...] = jnp.zeros(carry_s.shape, F32)

    qm = qm_ref[...]

    def update(s, v_list):
        m_prev = m_s[...]
        m_new = jnp.maximum(m_prev, jnp.max(s, axis=-1, keepdims=True))
        alpha = jnp.exp2(m_prev - m_new)
        p = jnp.exp2(s - m_new)
        l_s[...] = alpha * l_s[...] + jnp.sum(p, axis=-1, keepdims=True)
        pv = None
        for j, load_v in enumerate(v_list):
            t = _dot_nt(p[:, j * PAGE:(j + 1) * PAGE].astype(BF16), load_v().astype(BF16))
            pv = t if pv is None else pv + t
        acc_s[...] = alpha * acc_s[...] + pv
        m_s[...] = m_new

    def expand_rows(ck):
        return jnp.concatenate(
            [jnp.broadcast_to(ck[h:h + 1, :], (SUBLANES, PAGE)) for h in range(N_HEADS)], axis=0)

    def upper(block):
        r = lax.broadcasted_iota(I32, (PAGE, PAGE), 0)
        c = lax.broadcasted_iota(I32, (PAGE, PAGE), 1)
        keep = r <= c
        if block:
            keep = keep & (r // block == c // block)
        return jnp.where(keep, 1.0, 0.0).astype(BF16)

    s_parts = [_dot(qm, bufs[0][slot, j].astype(BF16)) for j in range(pages)]
    if mode == "fox":
        lf = jnp.concatenate([bufs[2][slot, j] for j in range(pages)], axis=0)
        cum = _dot_exact_rhs(lf, upper(0))
        carry = carry_s[...]
        for j in range(pages):
            cj = cum[j * SUBLANES:(j + 1) * SUBLANES]
            s_parts[j] = s_parts[j] - expand_rows((cj + carry) * LOG2E)
            carry = carry + cj[:, PAGE - 1:PAGE]
        carry_s[...] = carry
        s = jnp.concatenate(s_parts, axis=1)
    else:
        s = jnp.concatenate(s_parts, axis=1) + bias_ref[g]
    update(s, [functools.partial(lambda j: bufs[1][slot, j], j) for j in range(pages)])

    @pl.when(g == ng - 1)
    def _():
        sn = _dot(qm, knew_ref[...].astype(BF16))
        col = lax.broadcasted_iota(I32, (nrow, PAGE), 1)
        t_row = lax.broadcasted_iota(I32, (nrow, PAGE), 0) % SUBLANES
        rel = t_row - (col - s_idx * 4)
        ok = (rel >= 0) & (col >= s_idx * 4) & (t_row < 4)
        if mode == "fox":
            lfn = small_ref[...].T[SUBLANES:2 * SUBLANES, :]
            ckn = expand_rows((_dot_exact_rhs(lfn, upper(4)) + carry_s[...]) * LOG2E)
            cq = jnp.sum(jnp.where(rel == 0, ckn, 0.0), axis=-1, keepdims=True)
            m_s[...] = m_s[...] + cq
            sn = sn - ckn + cq
        else:
            bn = biasn_ref[...]
            add = jnp.zeros_like(sn)
            for r in range(4):
                add = jnp.where(rel == r, bn[:, r:r + 1], add)
            sn = sn + add
        sn = jnp.where(ok, sn, NEG_INF)
        update(sn, [lambda: vnew_ref[...]])

        o1 = acc_s[...] / l_s[...]
        if mode == "da":
            a = o1[0:4 * SUBLANES] - _lam(lamp_ref, lam_init) * o1[4 * SUBLANES:]
        else:
            a = o1
        lane = lax.broadcasted_iota(I32, (SUBLANES, ATT_W), 1)
        o8 = jnp.zeros((SUBLANES, ATT_W), F32)
        for h in range(N_HEADS):
            o8 = jnp.where((lane >= h * HEAD_W) & (lane < (h + 1) * HEAD_W),
                           a[h * SUBLANES:(h + 1) * SUBLANES], o8)
        y = _head_norm(o8, N_HEADS) * nw_ref[...]
        if mode == "da":
            y = y * (1.0 - lam_init)
        o_ref[...] = y[0:4] * _silu(gate_ref[...])


def _decode(mode, layer, page_table, qm, k_cache, v_cache, knew, vnew, gate, norm_w, extra,
            pages, lam_init=0.0):
    nseq, nrow, _ = qm.shape
    npages = page_table.shape[1]
    ng = npages // pages

    def const(shape):
        return pl.BlockSpec(shape, lambda t, pt: (0,) * len(shape))

    def per_seq(rows):
        return pl.BlockSpec((None, rows, ATT_W), lambda t, pt: (t // ng, 0, 0))

    hbm = pl.BlockSpec(memory_space=pl.ANY)
    in_specs = [per_seq(nrow), hbm, hbm]
    args = [qm, k_cache, v_cache]
    bufs = [pltpu.VMEM((2, pages, ATT_W, PAGE), F32), pltpu.VMEM((2, pages, ATT_W, PAGE), F32)]
    state = [pltpu.VMEM((nrow, 1), F32), pltpu.VMEM((nrow, 1), F32), pltpu.VMEM((nrow, ATT_W), F32)]
    if mode == "fox":
        lf_cache, small = extra
        in_specs += [hbm, const((PAGE, LANES))]
        args += [lf_cache, small]
        bufs.append(pltpu.VMEM((2, pages, SUBLANES, PAGE), F32))
        state.append(pltpu.VMEM((SUBLANES, PAGE), F32))
    else:
        bias, bias_new, lamp = extra
        in_specs += [const(bias.shape), const(bias_new.shape), const(lamp.shape)]
        args += [bias, bias_new, lamp]
    in_specs += [const((ATT_W, PAGE)), const((ATT_W, PAGE)), per_seq(4), const((1, ATT_W))]
    args += [knew, vnew, gate, norm_w]
    scratch = bufs + [pltpu.SemaphoreType.DMA((len(bufs), 2))] + state
    return pl.pallas_call(
        functools.partial(_decode_kernel, mode=mode, layer=layer, pages=pages, ng=ng, lam_init=lam_init),
        grid_spec=pltpu.PrefetchScalarGridSpec(
            num_scalar_prefetch=1, grid=(nseq * ng,), in_specs=in_specs,
            out_specs=per_seq(4), scratch_shapes=scratch),
        out_shape=jax.ShapeDtypeStruct((nseq, 4, ATT_W), F32),
        compiler_params=_cp(("arbitrary",)),
        name="decode_" + mode,
    )(page_table.reshape(-1), *args)


def _pad_lanes(v, width=LANES):
    return jnp.pad(v, ((0, 0), (0, width - v.shape[-1])))


def _masked_queries(q, ncomp):
    nseq = q.shape[0]
    width = HEAD_W // ncomp
    lane = jnp.arange(ATT_W)
    q8 = jnp.pad(q, ((0, 0), (0, SUBLANES - 4), (0, 0)))
    blocks = []
    for c in range(ncomp):
        for h in range(N_HEADS):
            lo = h * HEAD_W + c * width
            blocks.append(jnp.where((lane >= lo) & (lane < lo + width), q8, jnp.zeros_like(q8)))
    return jnp.concatenate(blocks, axis=1).reshape(nseq, ncomp * N_HEADS * SUBLANES, ATT_W)


def kernel(x_prompt, x_sample, c_prompt, c_sample, cache_da_k, cache_da_v, cache_fox_k, cache_fox_v,
           cache_fox_logf, state_ssm, state_conv, page_table, rel_bias, w_ada, b_ada, g_pre, g_post,
           w_in, w_out, da_lam_q1, da_lam_k1, da_lam_q2, da_lam_k2, da_subln, conv_w, conv_b, dt_bias,
           a_log, d_skip, ssm_norm, fox_fbias, fox_norm):
    depth = w_in.shape[0]
    bsz, seq, _ = x_prompt.shape
    nseq, dec_len, _ = x_sample.shape
    nphys = cache_da_k.shape[1]
    npages = page_table.shape[1]
    past = npages * PAGE
    assert dec_len == 4 and nseq * dec_len == PAGE
    tile = 512 if seq % 512 == 0 else 128
    chunk = 128
    pages = 32 if npages % 32 == 0 else npages

    def cols(a, b):
        return w_in[:, :, a:b]

    w_rows = jnp.concatenate([cols(C_DAK, C_DAV), cols(C_DAG, C_Z), cols(C_Z, C_XBC), cols(C_XBC, C_DT),
                              cols(C_FXK, C_FXV), cols(C_FXG, C_FXF)], axis=-1).astype(BF16)
    w_kvt = jnp.swapaxes(jnp.concatenate([cols(C_DAQ, C_DAG), cols(C_FXQ, C_FXG)], axis=-1), 1, 2).astype(BF16)
    w_small = jnp.pad(jnp.concatenate([cols(C_DT, C_FXQ), cols(C_FXF, C_FXF + 4)], axis=-1),
                      ((0, 0), (0, 0), (0, LANES - 12))).astype(BF16)
    small_bias = _pad_lanes(jnp.concatenate([dt_bias, fox_fbias], axis=-1))
    w_out_b = w_out.astype(BF16)
    lamp = jnp.pad(jnp.stack([da_lam_q1, da_lam_k1, da_lam_q2, da_lam_k2], axis=1),
                   ((0, 0), (0, SUBLANES - 4), (0, LANES - DA_QK)))
    conv_w8 = jnp.pad(conv_w, ((0, 0), (0, SUBLANES - CONV_W), (0, 0)))
    a_log_p = _pad_lanes(a_log)
    d_skip_e = jnp.repeat(d_skip, HEAD_W, axis=-1)
    da_nw2 = jnp.tile(da_subln, (1, 2))
    da_nw4 = jnp.tile(da_subln, (1, 4))
    fx_nw2 = jnp.tile(fox_norm, (1, 2))
    fx_nw4 = jnp.tile(fox_norm, (1, 4))

    def pages_t(c):
        return jnp.transpose(c, (0, 1, 3, 4, 2)).reshape(depth, nphys, ATT_W, PAGE)

    kc_da, vc_da, kc_fx, vc_fx = (pages_t(c) for c in (cache_da_k, cache_da_v, cache_fox_k, cache_fox_v))
    lf_cache = jnp.pad(jnp.transpose(cache_fox_logf, (0, 1, 3, 2)),
                       ((0, 0), (0, 0), (0, SUBLANES - N_HEADS), (0, 0)))
    conv_init_s = jnp.pad(state_conv, ((0, 0), (0, 0), (SUBLANES - 3, 0), (0, 0)))
    ssm_init_s = state_ssm.reshape(depth, nseq, 4, LANES, SSM_STATE)
    conv_init_p = jnp.zeros((bsz, SUBLANES, CONV_CH), F32)
    ssm_init_p = jnp.zeros((bsz, 4, LANES, SSM_STATE), F32)

    nc = bsz + nseq
    nc_pad = -(-nc // SUBLANES) * SUBLANES
    c_all = jnp.pad(jnp.concatenate([c_prompt, c_sample], axis=0), ((0, nc_pad - nc), (0, 0)))
    mod = _adaln(c_all, w_ada, b_ada)

    ii = jnp.arange(tile, dtype=I32)
    d = ii[None, :] - ii[:, None]
    rel_tiles = jnp.concatenate([d + 2 * tile, d + tile, d], axis=0)
    bias_tiles = _t5_bias(rel_bias, rel_tiles, tile).reshape(N_HEADS, 3, tile, tile) * LOG2E
    tt = jnp.arange(SUBLANES, dtype=I32)
    rel_dec = jnp.where(tt[:, None] < dec_len, past + tt[:, None] - jnp.arange(past, dtype=I32)[None, :], 1)
    bias_dec = _t5_bias(rel_bias, rel_dec, SUBLANES)
    bias_dec = jnp.tile(bias_dec.reshape(N_HEADS * SUBLANES, past), (2, 1))
    bias_dec = jnp.transpose(bias_dec.reshape(2 * N_HEADS * SUBLANES, npages // pages, pages * PAGE),
                             (1, 0, 2)) * LOG2E
    bias_new = jnp.tile(jnp.repeat(_pad_lanes(rel_bias[:4].T), SUBLANES, axis=0), (2, 1)) * LOG2E
    fox_mask = jnp.stack([jnp.zeros((tile, tile), F32), jnp.zeros((tile, tile), F32),
                          jnp.where(d >= 0, 0.0, NEG_INF).astype(F32)])[None]

    xp = x_prompt
    xs = x_sample.reshape(1, nseq * dec_len, D_MODEL)
    outs_p = [[] for _ in range(7)]
    outs_s = [[] for _ in range(7)]
    for l in range(depth):
        lam_init = 0.8 - 0.6 * math.exp(-0.3 * l)
        shift_p, scale_p, gate_p = (mod[l, :bsz, i * D_MODEL:(i + 1) * D_MODEL][:, None] for i in range(3))
        shift_s, scale_s, gate_s = (
            jnp.repeat(mod[l, bsz:nc, i * D_MODEL:(i + 1) * D_MODEL], dec_len, axis=0)[None] for i in range(3))
        gp = g_pre[l][None]
        sb = small_bias[l][None]

        (kdar, gda, z, xbc, small, kfxr, gfx, qdat, kda, vda, vdab, qfxt, kfx, vfx, vfxb) = _in_proj(
            xp, shift_p, scale_p, gp, w_rows[l], w_kvt[l], w_small[l], sb, tile)
        ckrep, cqrow = _cumsum(small, tile)
        o_da = _flash("da", qdat, kdar, vdab, gda, da_nw2[l][None], bias_tiles, (lamp[l],), tile, lam_init)
        o_fx = _flash("fox", qfxt, kfxr, vfxb, gfx, fx_nw2[l][None], fox_mask, (ckrep, cqrow), tile)
        o_ssm, ssm_fin = _ssd(xbc, z, small, conv_init_p, ssm_init_p, conv_w8[l], conv_b[l][None],
                              a_log_p[l][None], d_skip_e[l][None], ssm_norm[l][None], chunk, None)
        xp = _out_proj(o_da, o_ssm, o_fx, xp, gate_p, w_out_b[l], g_post[l][None], tile)

        def heads_t(a, n=bsz, s=seq):
            return jnp.transpose(a.reshape(n, N_HEADS, HEAD_W, s), (0, 3, 1, 2))

        for lst, val in zip(outs_p, (heads_t(kda), heads_t(vda), heads_t(kfx), heads_t(vfx),
                                     small[:, :, 8:12], ssm_fin.reshape(bsz, SSM_HEADS, HEAD_W, SSM_STATE),
                                     xbc[:, seq - 3:, :])):
            lst.append(val)

        (_, gda, z, xbc, small, _, gfx, qdat, kda, vda, _, qfxt, kfx, vfx, _) = _in_proj(
            xs, shift_s, scale_s, gp, w_rows[l], w_kvt[l], w_small[l], sb, PAGE)
        qm_da = _masked_queries(qdat[0].T.reshape(nseq, dec_len, ATT_W), 2)
        qm_fx = _masked_queries(qfxt[0].T.reshape(nseq, dec_len, ATT_W), 1)
        o_da = _decode("da", l, page_table, qm_da, kc_da, vc_da, kda[0], vda[0],
                       gda.reshape(nseq, dec_len, ATT_W), da_nw4[l][None],
                       (bias_dec, bias_new, lamp[l]), pages, lam_init)
        o_fx = _decode("fox", l, page_table, qm_fx, kc_fx, vc_fx, kfx[0], vfx[0],
                       gfx.reshape(nseq, dec_len, ATT_W), fx_nw4[l][None],
                       (lf_cache, small[0]), pages)

        def pad_seq(a):
            a = a.reshape(nseq, dec_len, a.shape[-1])
            return jnp.pad(a, ((0, 0), (0, chunk - dec_len), (0, 0)))

        o_ssm, ssm_fin = _ssd(pad_seq(xbc), pad_seq(z), pad_seq(small), conv_init_s[l], ssm_init_s[l],
                              conv_w8[l], conv_b[l][None], a_log_p[l][None], d_skip_e[l][None],
                              ssm_norm[l][None], chunk, dec_len)
        o_ssm = o_ssm[:, :dec_len].reshape(1, nseq * dec_len, SSM_W)
        xs = _out_proj(o_da.reshape(1, nseq * dec_len, ATT_W), o_ssm, o_fx.reshape(1, nseq * dec_len, ATT_W),
                       xs, gate_s, w_out_b[l], g_post[l][None], PAGE)

        def heads_s(a):
            return heads_t(a, 1, nseq * dec_len).reshape(nseq, dec_len, N_HEADS, HEAD_W)

        xbc_s = xbc.reshape(nseq, dec_len, CONV_CH)
        for lst, val in zip(outs_s, (heads_s(kda), heads_s(vda), heads_s(kfx), heads_s(vfx),
                                     small.reshape(nseq, dec_len, LANES)[:, :, 8:12],
                                     ssm_fin.reshape(nseq, SSM_HEADS, HEAD_W, SSM_STATE),
                                     xbc_s[:, dec_len - 3:, :])):
            lst.append(val)

    y_s = xs.reshape(nseq, dec_len, D_MODEL)
    return (xp, y_s) + tuple(jnp.stack(v) for v in outs_p) + tuple(jnp.stack(v) for v in outs_s)
```

```python
import functools
import math

import jax
import jax.numpy as jnp
from jax import lax
from jax.experimental import pallas as pl
from jax.experimental.pallas import tpu as pltpu

F32 = jnp.float32
BF16 = jnp.bfloat16
I32 = jnp.int32

EPS = 1e-6
NEG_INF = -1e30

D_MODEL = 1024
N_HEADS = 4
HEAD_W = 64
ATT_W = N_HEADS * HEAD_W
DA_QK = 32
SSM_HEADS = 8
SSM_W = 512
SSM_STATE = 128
CONV_W = 4
CONV_CH = 1024
N_BUCKETS = 32
MAX_DISTANCE = 128
PAGE = 128
LANES = 128
SUBLANES = 8
VMEM_LIMIT = 56 * 1024 * 1024

DA_SCALE = DA_QK ** -0.5
FOX_SCALE = HEAD_W ** -0.5
LOG2E = math.log2(math.e)

_SIZES = (256, 256, 256, 256, 512, 1024, 8, 256, 256, 256, 256, 4)
_OFFS = [0]
for _s in _SIZES:
    _OFFS.append(_OFFS[-1] + _s)
(C_DAQ, C_DAK, C_DAV, C_DAG, C_Z, C_XBC, C_DT, C_FXQ, C_FXK, C_FXV, C_FXG, C_FXF) = _OFFS[:12]


def _cp(sem):
    return pltpu.CompilerParams(dimension_semantics=sem, vmem_limit_bytes=VMEM_LIMIT)


def _silu(v):
    return v * (1.0 / (1.0 + jnp.exp(-v)))


def _split3(v):
    a = v.astype(BF16)
    r = v - a.astype(F32)
    b = r.astype(BF16)
    r = r - b.astype(F32)
    return a, b, r.astype(BF16)


def _dot(a, b):
    return jnp.dot(a, b, preferred_element_type=F32)


def _dot_nt(a, b):
    return lax.dot_general(a, b, (((1,), (1,)), ((), ())), preferred_element_type=F32)


def _dot_exact_rhs(v, w):
    a, b, c = _split3(v)
    return _dot(a, w) + _dot(b, w) + _dot(c, w)


def _dot_exact_lhs(w, v):
    a, b, c = _split3(v)
    return _dot(w, a) + _dot(w, b) + _dot(w, c)


def _adaln_kernel(c_ref, w_ref, b_ref, o_ref):
    c = c_ref[...]
    o_ref[...] = _dot(_silu(c), w_ref[...]) + b_ref[...]


def _adaln(c_all, w_ada, b_ada):
    depth = w_ada.shape[0]
    rows = c_all.shape[0]
    nblk = 3
    return pl.pallas_call(
        _adaln_kernel,
        grid=(depth, nblk),
        in_specs=[
            pl.BlockSpec((rows, D_MODEL), lambda l, j: (0, 0)),
            pl.BlockSpec((None, D_MODEL, D_MODEL), lambda l, j: (l, 0, j)),
            pl.BlockSpec((None, 1, D_MODEL), lambda l, j: (l, 0, j)),
        ],
        out_specs=pl.BlockSpec((None, rows, D_MODEL), lambda l, j: (l, 0, j)),
        out_shape=jax.ShapeDtypeStruct((depth, rows, 3 * D_MODEL), F32),
        compiler_params=_cp(("parallel", "parallel")),
        name="adaln",
    )(c_all, w_ada, b_ada.reshape(depth, 1, 3 * D_MODEL))


def _t5_kernel(tab_ref, rel_ref, o_ref):
    rel = rel_ref[...]
    n = jnp.maximum(rel, 0)
    max_exact = N_BUCKETS // 2
    nf = jnp.maximum(n, 1).astype(F32)
    large = max_exact + (jnp.log(nf / max_exact) / math.log(MAX_DISTANCE / max_exact)
                         * (N_BUCKETS - max_exact)).astype(I32)
    large = jnp.minimum(large, N_BUCKETS - 1)
    bucket = jnp.where(n < max_exact, n, large)
    for h in range(N_HEADS):
        acc = jnp.zeros(rel.shape, F32)
        for k in range(N_BUCKETS):
            acc = jnp.where(bucket == k, tab_ref[k, h], acc)
        o_ref[h] = jnp.where(rel >= 0, acc, NEG_INF)


def _t5_bias(rel_bias, rel, block_rows):
    rows, cols = rel.shape
    return pl.pallas_call(
        _t5_kernel,
        grid=(rows // block_rows,),
        in_specs=[
            pl.BlockSpec(memory_space=pltpu.SMEM),
            pl.BlockSpec((block_rows, cols), lambda i: (i, 0)),
        ],
        out_specs=pl.BlockSpec((N_HEADS, block_rows, cols), lambda i: (0, i, 0)),
        out_shape=jax.ShapeDtypeStruct((N_HEADS, rows, cols), F32),
        compiler_params=_cp(("parallel",)),
        name="t5_bias",
    )(rel_bias, rel)


def _in_proj_kernel(x_ref, shift_ref, scale_ref, g_ref, wr_ref, wt_ref, ws_ref, sb_ref, *refs):
    (kda_ref, gda_ref, z_ref, xbc_ref, small_ref, kfx_ref, gfx_ref,
     qdat_ref, kdat_ref, vdat_ref, vdatb_ref, qfxt_ref, kfxt_ref, vfxt_ref, vfxtb_ref) = refs[-15:]
    x = x_ref[...]
    y = x * lax.rsqrt(jnp.mean(x * x, axis=-1, keepdims=True) + EPS)
    h = (y * g_ref[...]) * (1.0 + scale_ref[...]) + shift_ref[...]
    hb = h.astype(BF16)

    def rows(a, b):
        return _dot(hb, wr_ref[:, a:b])

    kda_ref[...] = rows(0, 256).astype(BF16)
    gda_ref[...] = rows(256, 512)
    z_ref[...] = rows(512, 1024)
    xbc_ref[...] = rows(1024, 2048)
    kfx_ref[...] = rows(2048, 2304).astype(BF16)
    gfx_ref[...] = rows(2304, 2560)

    v = _dot(hb, ws_ref[...]) + sb_ref[...]
    t = jnp.log1p(jnp.exp(-jnp.abs(v)))
    lane = lax.broadcasted_iota(I32, v.shape, 1)
    small_ref[...] = jnp.where(lane < 8, jnp.maximum(v, 0.0) + t,
                               jnp.where(lane < 12, jnp.minimum(v, 0.0) - t, 0.0))

    def cols(j):
        return _dot_nt(wt_ref[j * ATT_W:(j + 1) * ATT_W, :], hb)

    qdat_ref[...] = (cols(0) * (DA_SCALE * LOG2E)).astype(BF16)
    kdat_ref[...] = cols(1)
    v = cols(2)
    vdat_ref[...] = v
    vdatb_ref[...] = v.astype(BF16)
    qfxt_ref[...] = (cols(3) * (FOX_SCALE * LOG2E)).astype(BF16)
    kfxt_ref[...] = cols(4)
    v = cols(5)
    vfxt_ref[...] = v
    vfxtb_ref[...] = v.astype(BF16)


def _layer_spec(arr, layer, grid_rank):
    rest = arr.shape[1:]
    return pl.BlockSpec((None,) + rest, lambda *g: (layer,) + (0,) * len(rest))


def _mod_spec(mod, layer, which, tm):
    if mod.shape[3] == 1:
        return pl.BlockSpec((None, None, None, 1, D_MODEL), lambda b, i: (layer, which, b, 0, 0))
    return pl.BlockSpec((None, None, None, tm, D_MODEL), lambda b, i: (layer, which, b, i, 0))


def _in_proj(x, mod, layer, g_pre, w_rows, w_kvt, w_small, small_bias, tm, stacked=None):
    bsz, seq, _ = x.shape
    nt = seq // tm

    def row_spec(width):
        return pl.BlockSpec((None, tm, width), lambda b, i: (b, i, 0))

    def row_shape(width, dt=F32):
        return jax.ShapeDtypeStruct((bsz, seq, width), dt)

    if stacked is None:
        t_spec = pl.BlockSpec((None, ATT_W, tm), lambda b, i: (b, 0, i))
        t_shape = jax.ShapeDtypeStruct((bsz, ATT_W, seq), F32)
        extra_in, extra_specs, aliases = [], [], {}
    else:
        t_spec = pl.BlockSpec((None, None, ATT_W, tm), lambda b, i: (layer, b, 0, i))
        t_shape = jax.ShapeDtypeStruct(stacked[0].shape, F32)
        extra_in = list(stacked)
        extra_specs = [pl.BlockSpec(memory_space=pl.ANY)] * 4
        aliases = {8: 8, 9: 9, 10: 12, 11: 13}
    tq_spec = pl.BlockSpec((None, ATT_W, tm), lambda b, i: (b, 0, i))
    tb_spec = pl.BlockSpec((None, None, ATT_W, tm), lambda b, i: (b, i, 0, 0))
    tq_shape = jax.ShapeDtypeStruct((bsz, ATT_W, seq), BF16)
    tb_shape = jax.ShapeDtypeStruct((bsz, nt, ATT_W, tm), BF16)
    params = (g_pre, w_rows, w_kvt, w_small, small_bias)
    return pl.pallas_call(
        _in_proj_kernel,
        grid=(bsz, nt),
        in_specs=[row_spec(D_MODEL), _mod_spec(mod, layer, 0, tm), _mod_spec(mod, layer, 1, tm)]
        + [_layer_spec(p, layer, 2) for p in params] + extra_specs,
        out_specs=[row_spec(256), row_spec(256), row_spec(512), row_spec(1024), row_spec(LANES),
                   row_spec(256), row_spec(256),
                   tq_spec, t_spec, t_spec, tb_spec, tq_spec, t_spec, t_spec, tb_spec],
        out_shape=[row_shape(256, BF16), row_shape(256), row_shape(512), row_shape(1024),
                   row_shape(LANES), row_shape(256, BF16), row_shape(256),
                   tq_shape, t_shape, t_shape, tb_shape, tq_shape, t_shape, t_shape, tb_shape],
        input_output_aliases=aliases,
        compiler_params=_cp(("parallel", "parallel")),
        name="in_proj",
    )(x, mod, mod, *params, *extra_in)


def _cumsum_kernel(x_ref, rep_ref, row_ref, carry_s):
    @pl.when(pl.program_id(1) == 0)
    def _():
        carry_s[...] = jnp.zeros_like(carry_s)

    x = x_ref[...]
    t = x.shape[0]
    r = lax.broadcasted_iota(I32, (t, t), 0)
    c = lax.broadcasted_iota(I32, (t, t), 1)
    tri = jnp.where(r >= c, 1.0, 0.0).astype(BF16)
    cum = _dot_exact_lhs(tri, x) + carry_s[...]
    carry_s[...] = cum[t - 1:t, :]
    sr = lax.broadcasted_iota(I32, (LANES, N_HEADS * LANES), 0)
    sc = lax.broadcasted_iota(I32, (LANES, N_HEADS * LANES), 1)
    spread = jnp.where(sr == SUBLANES + sc // LANES, 1.0, 0.0).astype(BF16)
    rep_ref[...] = _dot_exact_rhs(cum, spread) * LOG2E
    row_ref[...] = cum.T[SUBLANES:2 * SUBLANES, :] * LOG2E


def _cumsum(small, tm):
    bsz, seq, _ = small.shape
    return pl.pallas_call(
        _cumsum_kernel,
        grid=(bsz, seq // tm),
        in_specs=[pl.BlockSpec((None, tm, LANES), lambda b, i: (b, i, 0))],
        out_specs=[pl.BlockSpec((None, tm, N_HEADS * LANES), lambda b, i: (b, i, 0)),
                   pl.BlockSpec((None, None, SUBLANES, tm), lambda b, i: (b, i, 0, 0))],
        out_shape=[jax.ShapeDtypeStruct((bsz, seq, N_HEADS * LANES), F32),
                   jax.ShapeDtypeStruct((bsz, seq // tm, SUBLANES, tm), F32)],
        scratch_shapes=[pltpu.VMEM((1, LANES), F32)],
        compiler_params=_cp(("parallel", "arbitrary")),
        name="logf_cumsum",
    )(small)


def _lam(lamp_ref, lam_init):
    p = lamp_ref[...]
    s1 = jnp.sum(p[0:1] * p[1:2], axis=-1, keepdims=True)
    s2 = jnp.sum(p[2:3] * p[3:4], axis=-1, keepdims=True)
    return jnp.exp(s1) - jnp.exp(s2) + lam_init


def _head_norm(o, n_heads):
    lane = lax.broadcasted_iota(I32, o.shape, 1)
    o2 = o * o
    rs = jnp.zeros_like(o)
    for h in range(n_heads):
        m = (lane >= h * HEAD_W) & (lane < (h + 1) * HEAD_W)
        ss = jnp.sum(jnp.where(m, o2, 0.0), axis=-1, keepdims=True)
        rs = jnp.where(m, lax.rsqrt(ss / HEAD_W + EPS), rs)
    return o * rs


def _flash_kernel(*refs, mode, tile, lam_init):
    if mode == "da":
        (qt_ref, k_ref, v_ref, gate_ref, nw_ref, bias_ref, lamp_ref,
         o_ref, qm_s, m_s, mb_s, acc_s, s_s) = refs
        ncomp = 2
    else:
        (qt_ref, k_ref, v_ref, gate_ref, nw_ref, bias_ref, ckrep_ref, cqrow_ref,
         o_ref, qm_s, m_s, mb_s, acc_s, s_s, cq_s) = refs
        ncomp = 1
    hp = pl.program_id(1)
    qi = pl.program_id(2)
    nr = 2 * ncomp
    width = HEAD_W // ncomp
    row = lax.broadcasted_iota(I32, (LANES, tile), 0)

    qt = qt_ref[...]
    for r in range(nr):
        sel = (row >= r * width) & (row < (r + 1) * width)
        qm_s[r] = jnp.where(sel, qt, jnp.zeros_like(qt))
        m_s[r] = jnp.full((1, tile), NEG_INF, F32)
        acc_s[r] = jnp.zeros((LANES, tile), F32)
    if mode == "fox":
        for hh in range(2):
            cq_s[hh] = cqrow_ref[pl.ds(hp * 2 + hh, 1), :]

    def stage1(kj, slot):
        ks = pl.multiple_of(kj * tile, tile)
        kt = k_ref[pl.ds(ks, tile), :]
        kind = jnp.clip(kj - qi + 2, 0, 2)
        for r in range(nr):
            hh = r // ncomp
            s = _dot(kt, qm_s[r])
            if mode == "da":
                s = s + bias_ref[hh, kind]
                shift = 0.0
            else:
                ck = ckrep_ref[pl.ds(ks, tile), hh * LANES:(hh + 1) * LANES]
                s = s + bias_ref[0, kind] - jnp.tile(ck, (1, tile // LANES))
                shift = cq_s[hh]
            m_prev = m_s[r]
            m_new = jnp.maximum(m_prev, jnp.max(s, axis=0, keepdims=True) + shift)
            s_s[slot, r] = s
            mb_s[r] = m_prev
            m_s[r] = m_new

    def stage2_inputs():
        return [(mb_s[r], m_s[r]) for r in range(nr)]

    def stage2(kj, slot, pend):
        vt = v_ref[kj]
        for r in range(nr):
            hh = r // ncomp
            m_prev, m_new = pend[r]
            shift = cq_s[hh] if mode == "fox" else 0.0
            alpha = jnp.exp2(m_prev - m_new)
            p = jnp.exp2(s_s[slot, r] - (m_new - shift)).astype(BF16)
            own = (row >= hh * HEAD_W) & (row < (hh + 1) * HEAD_W)
            vmod = jnp.where(own, vt, jnp.ones_like(vt))
            acc_s[r] = alpha * acc_s[r] + _dot(vmod, p)

    def body(kj, carry):
        for par in range(2):
            @pl.when(kj % 2 == par)
            def _():
                pend = stage2_inputs()
                stage1(kj, par)
                stage2(kj - 1, 1 - par, pend)
        return carry

    stage1(0, 0)
    lax.fori_loop(1, qi + 1, body, 0)
    for par in range(2):
        @pl.when(qi % 2 == par)
        def _():
            stage2(qi, par, stage2_inputs())

    outs = []
    for hh in range(2):
        den = HEAD_W * (1 - hh)
        if mode == "da":
            a0 = acc_s[2 * hh]
            a1 = acc_s[2 * hh + 1]
            outs.append(a0 / a0[den:den + 1, :] - _lam(lamp_ref, lam_init) * (a1 / a1[den:den + 1, :]))
        else:
            a0 = acc_s[hh]
            outs.append(a0 / a0[den:den + 1, :])
    o = jnp.where(row < HEAD_W, outs[0], outs[1]).T
    y = _head_norm(o, 2) * nw_ref[...]
    if mode == "da":
        y = y * (1.0 - lam_init)
    o_ref[...] = y * _silu(gate_ref[...])


def _flash(mode, layer, qt, k, vt, gate, norm_w, bias, extra, tile, lam_init=0.0):
    bsz, seq, _ = k.shape
    nq = seq // tile
    ncomp = 2 if mode == "da" else 1
    nr = 2 * ncomp
    blk = pl.BlockSpec((None, tile, LANES), lambda b, p, i: (b, i, p))
    qt_spec = pl.BlockSpec((None, LANES, tile), lambda b, p, i: (b, p, i))
    k_spec = pl.BlockSpec((None, seq, LANES), lambda b, p, i: (b, 0, p))
    v_spec = pl.BlockSpec((None, nq, LANES, tile), lambda b, p, i: (b, 0, p, 0))
    in_specs = [qt_spec, k_spec, v_spec, blk, _layer_spec(norm_w, layer, 3)]
    scratch = [pltpu.VMEM((nr, LANES, tile), BF16), pltpu.VMEM((nr, 1, tile), F32),
               pltpu.VMEM((nr, 1, tile), F32), pltpu.VMEM((nr, LANES, tile), F32),
               pltpu.VMEM((2, nr, tile, tile), F32)]
    if mode == "da":
        in_specs += [pl.BlockSpec((2, 3, tile, tile), lambda b, p, i: (p, 0, 0, 0)),
                     _layer_spec(extra[0], layer, 3)]
    else:
        in_specs += [pl.BlockSpec((1, 3, tile, tile), lambda b, p, i: (0, 0, 0, 0)),
                     pl.BlockSpec((None, seq, 2 * LANES), lambda b, p, i: (b, 0, p)),
                     pl.BlockSpec((None, None, SUBLANES, tile), lambda b, p, i: (b, i, 0, 0))]
        scratch.append(pltpu.VMEM((2, 1, tile), F32))
    return pl.pallas_call(
        functools.partial(_flash_kernel, mode=mode, tile=tile, lam_init=lam_init),
        grid=(bsz, 2, nq),
        in_specs=in_specs,
        out_specs=blk,
        out_shape=jax.ShapeDtypeStruct((bsz, seq, ATT_W), F32),
        scratch_shapes=scratch,
        compiler_params=_cp(("parallel", "parallel", "arbitrary")),
        name="flash_" + mode,
    )(qt, k, vt, gate, norm_w, bias, *extra)


def _ssd_kernel(xbc_ref, z_ref, small_ref, cinit_ref, sinit_ref, cw_ref, cb_ref, alog_ref,
                dskip_ref, nw_ref, o_ref, sfin_ref, state_s, tail_s, *, chunk, valid_len):
    c = pl.program_id(1)
    nchunks = pl.num_programs(1)

    @pl.when(c == 0)
    def _():
        state_s[...] = sinit_ref[...]
        tail_s[...] = cinit_ref[...]

    x = xbc_ref[...]
    xcat = jnp.concatenate([tail_s[...], x], axis=0)
    conv = cb_ref[...] + xcat[SUBLANES - 3:SUBLANES - 3 + chunk] * cw_ref[0:1, :]
    for j in range(1, CONV_W):
        conv = conv + xcat[SUBLANES - 3 + j:SUBLANES - 3 + j + chunk] * cw_ref[j:j + 1, :]
    tail_s[...] = x[chunk - SUBLANES:chunk]
    xc = _silu(conv)
    xs = xc[:, :SSM_W]
    bm = xc[:, SSM_W:SSM_W + 2 * SSM_STATE].astype(BF16)
    cm = xc[:, SSM_W + 2 * SSM_STATE:].astype(BF16)

    lane = lax.broadcasted_iota(I32, (chunk, LANES), 1)
    row = lax.broadcasted_iota(I32, (chunk, LANES), 0)
    dtv = jnp.where(lane < SSM_HEADS, small_ref[...], 0.0)
    if valid_len is not None:
        dtv = jnp.where(row + c * chunk < valid_len, dtv, 0.0)
    d_a = dtv * (-jnp.exp(alog_ref[...]))

    ri = lax.broadcasted_iota(I32, (chunk, chunk), 0)
    ci = lax.broadcasted_iota(I32, (chunk, chunk), 1)
    causal = ri >= ci
    tri = jnp.where(causal, 1.0, 0.0).astype(BF16)
    a_cs = _dot_exact_lhs(tri, d_a)
    a_cs_t = a_cs.T

    er = lax.broadcasted_iota(I32, (LANES, SSM_W), 0)
    ec = lax.broadcasted_iota(I32, (LANES, SSM_W), 1)
    expand = jnp.where((ec >= er * HEAD_W) & (ec < (er + 1) * HEAD_W), 1.0, 0.0).astype(BF16)
    a_exp = _dot_exact_rhs(a_cs, expand)
    dt_exp = _dot_exact_rhs(dtv, expand)
    tot_exp = a_exp[chunk - 1:chunk, :]
    e_col = jnp.exp(a_exp)
    xdt = xs * dt_exp
    xw = xdt * jnp.exp(tot_exp - a_exp)

    srow = lax.broadcasted_iota(I32, (LANES, 1), 0)
    ys = []
    for g in range(2):
        bg = bm[:, g * SSM_STATE:(g + 1) * SSM_STATE]
        cg = cm[:, g * SSM_STATE:(g + 1) * SSM_STATE]
        scores = _dot_nt(cg, bg)
        for pp in range(2):
            p = g * 2 + pp
            sl = slice(p * LANES, (p + 1) * LANES)
            xdt_p = xdt[:, sl].astype(BF16)
            yd = []
            for h in (2 * p, 2 * p + 1):
                seg = a_cs[:, h:h + 1] - a_cs_t[h:h + 1, :]
                decay = jnp.exp(jnp.where(causal, seg, NEG_INF))
                yd.append(_dot((scores * decay).astype(BF16), xdt_p))
            y_diag = jnp.where(lane < HEAD_W, yd[0], yd[1])
            prev = state_s[p]
            y_off = _dot_nt(cg, prev.astype(BF16)) * e_col[:, sl]
            cs = _dot(xw[:, sl].T.astype(BF16), bg)
            e_tot = jnp.where(srow < HEAD_W, jnp.exp(a_cs[chunk - 1:chunk, 2 * p:2 * p + 1]),
                              jnp.exp(a_cs[chunk - 1:chunk, 2 * p + 1:2 * p + 2]))
            state_s[p] = prev * e_tot + cs
            ys.append(y_diag + y_off + dskip_ref[:, sl] * xs[:, sl])
    y = jnp.concatenate(ys, axis=1)
    yz = y * _silu(z_ref[...])
    o_ref[...] = yz * lax.rsqrt(jnp.mean(yz * yz, axis=-1, keepdims=True) + EPS) * nw_ref[...]

    @pl.when(c == nchunks - 1)
    def _():
        sfin_ref[...] = state_s[...]


def _ssd(layer, xbc, z, small, conv_init, ssm_init, conv_w, conv_b, a_log, d_skip, norm_w, chunk, valid_len):
    bsz, seq, _ = xbc.shape

    def row_spec(width):
        return pl.BlockSpec((None, chunk, width), lambda b, c: (b, c, 0))

    def state_spec(arr, shape):
        nd = len(shape)
        if arr.ndim == nd + 2:
            return pl.BlockSpec((None, None) + shape, lambda b, c: (layer, b) + (0,) * nd)
        return pl.BlockSpec((None,) + shape, lambda b, c: (b,) + (0,) * nd)

    st_spec = pl.BlockSpec((None, 4, LANES, SSM_STATE), lambda b, c: (b, 0, 0, 0))
    params = (conv_w, conv_b, a_log, d_skip, norm_w)
    return pl.pallas_call(
        functools.partial(_ssd_kernel, chunk=chunk, valid_len=valid_len),
        grid=(bsz, seq // chunk),
        in_specs=[row_spec(CONV_CH), row_spec(SSM_W), row_spec(LANES),
                  state_spec(conv_init, (SUBLANES, CONV_CH)), state_spec(ssm_init, (4, LANES, SSM_STATE))]
        + [_layer_spec(p, layer, 2) for p in params],
        out_specs=[row_spec(SSM_W), st_spec],
        out_shape=[jax.ShapeDtypeStruct((bsz, seq, SSM_W), F32),
                   jax.ShapeDtypeStruct((bsz, 4, LANES, SSM_STATE), F32)],
        scratch_shapes=[pltpu.VMEM((4, LANES, SSM_STATE), F32), pltpu.VMEM((SUBLANES, CONV_CH), F32)],
        compiler_params=_cp(("parallel", "arbitrary")),
        name="ssd",
    )(xbc, z, small, conv_init, ssm_init, conv_w, conv_b, a_log, d_skip, norm_w)


def _out_proj_kernel(oda_ref, ossm_ref, ofx_ref, x_ref, gate_ref, w_ref, g_ref, y_ref):
    mix = (_dot(oda_ref[...].astype(BF16), w_ref[0:ATT_W, :])
           + _dot(ossm_ref[...].astype(BF16), w_ref[ATT_W:ATT_W + SSM_W, :])
           + _dot(ofx_ref[...].astype(BF16), w_ref[ATT_W + SSM_W:, :]))
    n = mix * lax.rsqrt(jnp.mean(mix * mix, axis=-1, keepdims=True) + EPS) * g_ref[...]
    y_ref[...] = x_ref[...] + gate_ref[...] * n


def _out_proj(layer, o_da, o_ssm, o_fx, x, mod, w_out, g_post, tm):
    bsz, seq, _ = x.shape

    def row_spec(width):
        return pl.BlockSpec((None, tm, width), lambda b, i: (b, i, 0))

    return pl.pallas_call(
        _out_proj_kernel,
        grid=(bsz, seq // tm),
        in_specs=[row_spec(ATT_W), row_spec(SSM_W), row_spec(ATT_W), row_spec(D_MODEL),
                  _mod_spec(mod, layer, 2, tm), _layer_spec(w_out, layer, 2), _layer_spec(g_post, layer, 2)],
        out_specs=row_spec(D_MODEL),
        out_shape=jax.ShapeDtypeStruct((bsz, seq, D_MODEL), F32),
        compiler_params=_cp(("parallel", "parallel")),
        name="out_proj",
    )(o_da, o_ssm, o_fx, x, mod, w_out, g_post)


def _decode_kernel(pt_ref, *refs, mode, layer, pages, ng, lam_init):
    it = iter(refs)
    qm_ref = next(it)
    caches = [next(it), next(it)]
    if mode == "fox":
        caches.append(next(it))
        small_ref = next(it)
    else:
        bias_ref = next(it)
        biasn_ref = next(it)
        lamp_ref = next(it)
    knew_ref = next(it)
    vnew_ref = next(it)
    gate_ref = next(it)
    nw_ref = next(it)
    o_ref = next(it)
    bufs = [next(it) for _ in caches]
    sem = next(it)
    m_s = next(it)
    l_s = next(it)
    acc_s = next(it)
    carry_s = next(it) if mode == "fox" else None

    step = pl.program_id(0)
    nsteps = pl.num_programs(0)
    s_idx = step // ng
    g = step % ng
    slot = step % 2
    nrow = qm_ref.shape[0]

    def page_copy(kind, t, sl, j):
        page = pt_ref[t * pages + j]
        return pltpu.make_async_copy(caches[kind].at[layer, page], bufs[kind].at[sl, j], sem.at[kind, sl])

    def start_fetch(t, sl):
        for kind in range(len(caches)):
            for j in range(pages):
                page_copy(kind, t, sl, j).start()

    @pl.when(step == 0)
    def _():
        start_fetch(step, slot)

    @pl.when(step + 1 < nsteps)
    def _():
        start_fetch(step + 1, 1 - slot)

    for kind in range(len(caches)):
        for j in range(pages):
            page_copy(kind, step, slot, j).wait()

    @pl.when(g == 0)
    def _():
        m_s[...] = jnp.full(m_s.shape, NEG_INF, F32)
        l_s[...] = jnp.zeros(l_s.shape, F32)
        acc_s[...] = jnp.zeros(acc_s.shape, F32)
        if mode == "fox":
            carry_s[...] = jnp.zeros(carry_s.shape, F32)

    qm = qm_ref[...]

    def update(s, v_list):
        m_prev = m_s[...]
        m_new = jnp.maximum(m_prev, jnp.max(s, axis=-1, keepdims=True))
        alpha = jnp.exp2(m_prev - m_new)
        p = jnp.exp2(s - m_new)
        l_s[...] = alpha * l_s[...] + jnp.sum(p, axis=-1, keepdims=True)
        pv = None
        for j, load_v in enumerate(v_list):
            t = _dot_nt(p[:, j * PAGE:(j + 1) * PAGE].astype(BF16), load_v().astype(BF16))
            pv = t if pv is None else pv + t
        acc_s[...] = alpha * acc_s[...] + pv
        m_s[...] = m_new

    def expand_rows(ck):
        return jnp.concatenate(
            [jnp.broadcast_to(ck[h:h + 1, :], (SUBLANES, PAGE)) for h in range(N_HEADS)], axis=0)

    def upper(block):
        r = lax.broadcasted_iota(I32, (PAGE, PAGE), 0)
        c = lax.broadcasted_iota(I32, (PAGE, PAGE), 1)
        keep = r <= c
        if block:
            keep = keep & (r // block == c // block)
        return jnp.where(keep, 1.0, 0.0).astype(BF16)

    s_parts = [_dot(qm, bufs[0][slot, j].astype(BF16)) for j in range(pages)]
    if mode == "fox":
        lf = jnp.concatenate([bufs[2][slot, j] for j in range(pages)], axis=0)
        cum = _dot_exact_rhs(lf, upper(0))
        carry = carry_s[...]
        for j in range(pages):
            cj = cum[j * SUBLANES:(j + 1) * SUBLANES]
            s_parts[j] = s_parts[j] - expand_rows((cj + carry) * LOG2E)
            carry = carry + cj[:, PAGE - 1:PAGE]
        carry_s[...] = carry
        s = jnp.concatenate(s_parts, axis=1)
    else:
        s = jnp.concatenate(s_parts, axis=1) + bias_ref[g]
    update(s, [functools.partial(lambda j: bufs[1][slot, j], j) for j in range(pages)])

    @pl.when(g == ng - 1)
    def _():
        sn = _dot(qm, knew_ref[...].astype(BF16))
        col = lax.broadcasted_iota(I32, (nrow, PAGE), 1)
        t_row = lax.broadcasted_iota(I32, (nrow, PAGE), 0) % SUBLANES
        rel = t_row - (col - s_idx * 4)
        ok = (rel >= 0) & (col >= s_idx * 4) & (t_row < 4)
        if mode == "fox":
            lfn = small_ref[...].T[SUBLANES:2 * SUBLANES, :]
            ckn = expand_rows((_dot_exact_rhs(lfn, upper(4)) + carry_s[...]) * LOG2E)
            cq = jnp.sum(jnp.where(rel == 0, ckn, 0.0), axis=-1, keepdims=True)
            m_s[...] = m_s[...] + cq
            sn = sn - ckn + cq
        else:
            bn = biasn_ref[...]
            add = jnp.zeros_like(sn)
            for r in range(4):
                add = jnp.where(rel == r, bn[:, r:r + 1], add)
            sn = sn + add
        sn = jnp.where(ok, sn, NEG_INF)
        update(sn, [lambda: vnew_ref[...]])

        o1 = acc_s[...] / l_s[...]
        if mode == "da":
            a = o1[0:4 * SUBLANES] - _lam(lamp_ref, lam_init) * o1[4 * SUBLANES:]
        else:
            a = o1
        lane = lax.broadcasted_iota(I32, (SUBLANES, ATT_W), 1)
        o8 = jnp.zeros((SUBLANES, ATT_W), F32)
        for h in range(N_HEADS):
            o8 = jnp.where((lane >= h * HEAD_W) & (lane < (h + 1) * HEAD_W),
                           a[h * SUBLANES:(h + 1) * SUBLANES], o8)
        y = _head_norm(o8, N_HEADS) * nw_ref[...]
        if mode == "da":
            y = y * (1.0 - lam_init)
        o_ref[...] = y[0:4] * _silu(gate_ref[...])


def _decode(mode, layer, page_table, qm, k_cache, v_cache, knew, vnew, gate, norm_w, extra,
            pages, lam_init=0.0):
    nseq, nrow, _ = qm.shape
    npages = page_table.shape[0] // nseq
    ng = npages // pages

    def const(shape):
        return pl.BlockSpec(shape, lambda t, pt: (0,) * len(shape))

    def first(shape):
        return pl.BlockSpec((None,) + shape, lambda t, pt: (0,) * (len(shape) + 1))

    def per_seq(rows):
        return pl.BlockSpec((None, rows, ATT_W), lambda t, pt: (t // ng, 0, 0))

    hbm = pl.BlockSpec(memory_space=pl.ANY)
    in_specs = [per_seq(nrow), hbm, hbm]
    args = [qm, k_cache, v_cache]
    bufs = [pltpu.VMEM((2, pages, ATT_W, PAGE), F32), pltpu.VMEM((2, pages, ATT_W, PAGE), F32)]
    state = [pltpu.VMEM((nrow, 1), F32), pltpu.VMEM((nrow, 1), F32), pltpu.VMEM((nrow, ATT_W), F32)]
    if mode == "fox":
        lf_cache, small = extra
        in_specs += [hbm, first((PAGE, LANES))]
        args += [lf_cache, small]
        bufs.append(pltpu.VMEM((2, pages, SUBLANES, PAGE), F32))
        state.append(pltpu.VMEM((SUBLANES, PAGE), F32))
    else:
        bias, bias_new, lamp = extra
        in_specs += [const(bias.shape), const(bias_new.shape), _layer_spec(lamp, layer, 1)]
        args += [bias, bias_new, lamp]
    in_specs += [first((ATT_W, PAGE)), first((ATT_W, PAGE)), per_seq(4), _layer_spec(norm_w, layer, 1)]
    args += [knew, vnew, gate, norm_w]
    scratch = bufs + [pltpu.SemaphoreType.DMA((len(bufs), 2))] + state
    return pl.pallas_call(
        functools.partial(_decode_kernel, mode=mode, layer=layer, pages=pages, ng=ng, lam_init=lam_init),
        grid_spec=pltpu.PrefetchScalarGridSpec(
            num_scalar_prefetch=1, grid=(nseq * ng,), in_specs=in_specs,
            out_specs=per_seq(4), scratch_shapes=scratch),
        out_shape=jax.ShapeDtypeStruct((nseq, 4, ATT_W), F32),
        compiler_params=_cp(("arbitrary",)),
        name="decode_" + mode,
    )(page_table, *args)


def _pad_lanes(v, width=LANES):
    return jnp.pad(v, ((0, 0), (0, width - v.shape[-1])))


def _masked_queries(q, ncomp):
    nseq = q.shape[0]
    width = HEAD_W // ncomp
    lane = jnp.arange(ATT_W)
    q8 = jnp.pad(q, ((0, 0), (0, SUBLANES - 4), (0, 0)))
    blocks = []
    for c in range(ncomp):
        for h in range(N_HEADS):
            lo = h * HEAD_W + c * width
            blocks.append(jnp.where((lane >= lo) & (lane < lo + width), q8, jnp.zeros_like(q8)))
    return jnp.concatenate(blocks, axis=1).reshape(nseq, ncomp * N_HEADS * SUBLANES, ATT_W)


def kernel(x_prompt, x_sample, c_prompt, c_sample, cache_da_k, cache_da_v, cache_fox_k, cache_fox_v,
           cache_fox_logf, state_ssm, state_conv, page_table, rel_bias, w_ada, b_ada, g_pre, g_post,
           w_in, w_out, da_lam_q1, da_lam_k1, da_lam_q2, da_lam_k2, da_subln, conv_w, conv_b, dt_bias,
           a_log, d_skip, ssm_norm, fox_fbias, fox_norm):
    depth = w_in.shape[0]
    bsz, seq, _ = x_prompt.shape
    nseq, dec_len, _ = x_sample.shape
    nphys = cache_da_k.shape[1]
    npages = page_table.shape[1]
    past = npages * PAGE
    assert dec_len == 4 and nseq * dec_len == PAGE
    tile = 512 if seq % 512 == 0 else 128
    chunk = 128
    pages = 32 if npages % 32 == 0 else npages

    def cols(a, b):
        return w_in[:, :, a:b]

    w_rows = jnp.concatenate([cols(C_DAK, C_DAV), cols(C_DAG, C_Z), cols(C_Z, C_XBC), cols(C_XBC, C_DT),
                              cols(C_FXK, C_FXV), cols(C_FXG, C_FXF)], axis=-1).astype(BF16)
    w_kvt = jnp.swapaxes(jnp.concatenate([cols(C_DAQ, C_DAG), cols(C_FXQ, C_FXG)], axis=-1), 1, 2).astype(BF16)
    w_small = jnp.pad(jnp.concatenate([cols(C_DT, C_FXQ), cols(C_FXF, C_FXF + 4)], axis=-1),
                      ((0, 0), (0, 0), (0, LANES - 12))).astype(BF16)
    small_bias = _pad_lanes(jnp.concatenate([dt_bias, fox_fbias], axis=-1))
    w_out_b = w_out.astype(BF16)
    lamp = jnp.pad(jnp.stack([da_lam_q1, da_lam_k1, da_lam_q2, da_lam_k2], axis=1),
                   ((0, 0), (0, SUBLANES - 4), (0, LANES - DA_QK)))
    conv_w8 = jnp.pad(conv_w, ((0, 0), (0, SUBLANES - CONV_W), (0, 0)))
    a_log_p = _pad_lanes(a_log)
    d_skip_e = jnp.repeat(d_skip, HEAD_W, axis=-1)
    da_nw2 = jnp.tile(da_subln, (1, 2))
    da_nw4 = jnp.tile(da_subln, (1, 4))
    fx_nw2 = jnp.tile(fox_norm, (1, 2))
    fx_nw4 = jnp.tile(fox_norm, (1, 4))

    def pages_t(c):
        return jnp.transpose(c, (0, 1, 3, 4, 2)).reshape(depth, nphys, ATT_W, PAGE)

    kc_da, vc_da, kc_fx, vc_fx = (pages_t(c) for c in (cache_da_k, cache_da_v, cache_fox_k, cache_fox_v))
    lf_cache = jnp.pad(jnp.transpose(cache_fox_logf, (0, 1, 3, 2)),
                       ((0, 0), (0, 0), (0, SUBLANES - N_HEADS), (0, 0)))
    conv_init_s = jnp.pad(state_conv, ((0, 0), (0, 0), (SUBLANES - 3, 0), (0, 0)))
    ssm_init_s = state_ssm.reshape(depth, nseq, 4, LANES, SSM_STATE)
    conv_init_p = jnp.zeros((bsz, SUBLANES, CONV_CH), F32)
    ssm_init_p = jnp.zeros((bsz, 4, LANES, SSM_STATE), F32)

    nc = bsz + nseq
    nc_pad = -(-nc // SUBLANES) * SUBLANES
    c_all = jnp.pad(jnp.concatenate([c_prompt, c_sample], axis=0), ((0, nc_pad - nc), (0, 0)))
    mod = _adaln(c_all, w_ada, b_ada)

    ii = jnp.arange(tile, dtype=I32)
    d = ii[None, :] - ii[:, None]
    rel_tiles = jnp.concatenate([d + 2 * tile, d + tile, d], axis=0)
    bias_tiles = _t5_bias(rel_bias, rel_tiles, tile).reshape(N_HEADS, 3, tile, tile) * LOG2E
    tt = jnp.arange(SUBLANES, dtype=I32)
    rel_dec = jnp.where(tt[:, None] < dec_len, past + tt[:, None] - jnp.arange(past, dtype=I32)[None, :], 1)
    bias_dec = _t5_bias(rel_bias, rel_dec, SUBLANES)
    bias_dec = jnp.tile(bias_dec.reshape(N_HEADS * SUBLANES, past), (2, 1))
    bias_dec = jnp.transpose(bias_dec.reshape(2 * N_HEADS * SUBLANES, npages // pages, pages * PAGE),
                             (1, 0, 2)) * LOG2E
    bias_new = jnp.tile(jnp.repeat(_pad_lanes(rel_bias[:4].T), SUBLANES, axis=0), (2, 1)) * LOG2E
    fox_mask = jnp.stack([jnp.zeros((tile, tile), F32), jnp.zeros((tile, tile), F32),
                          jnp.where(d >= 0, 0.0, NEG_INF).astype(F32)])[None]

    mod_p = jnp.transpose(mod[:, :bsz].reshape(depth, bsz, 3, 1, D_MODEL), (0, 2, 1, 3, 4))
    mod_s = jnp.transpose(jnp.repeat(mod[:, bsz:nc], dec_len, axis=1).reshape(depth, 1, nseq * dec_len, 3, D_MODEL),
                          (0, 3, 1, 2, 4))
    proj_params = (g_pre[:, None], w_rows, w_kvt, w_small, small_bias[:, None])
    ssd_params = (conv_w8, conv_b[:, None], a_log_p[:, None], d_skip_e[:, None], ssm_norm[:, None])
    da_nw2, da_nw4, fx_nw2, fx_nw4 = (a[:, None] for a in (da_nw2, da_nw4, fx_nw2, fx_nw4))
    g_post3 = g_post[:, None]
    pt_flat = page_table.reshape(-1)

    xp = x_prompt
    xs = x_sample.reshape(1, nseq * dec_len, D_MODEL)
    stacks = tuple(jnp.zeros((depth, bsz, ATT_W, seq), F32) for _ in range(4))
    rest_p = [[] for _ in range(3)]
    outs_s = [[] for _ in range(7)]
    for l in range(depth):
        lam_init = 0.8 - 0.6 * math.exp(-0.3 * l)

        res = _in_proj(xp, mod_p, l, *proj_params, tile, stacked=stacks)
        kdar, gda, z, xbc, small, kfxr, gfx, qdat, _, _, vdab, qfxt, _, _, vfxb = res
        stacks = (res[8], res[9], res[12], res[13])
        ckrep, cqrow = _cumsum(small, tile)
        o_da = _flash("da", l, qdat, kdar, vdab, gda, da_nw2, bias_tiles, (lamp,), tile, lam_init)
        o_fx = _flash("fox", l, qfxt, kfxr, vfxb, gfx, fx_nw2, fox_mask, (ckrep, cqrow), tile)
        o_ssm, ssm_fin = _ssd(l, xbc, z, small, conv_init_p, ssm_init_p, *ssd_params, chunk, None)
        xp = _out_proj(l, o_da, o_ssm, o_fx, xp, mod_p, w_out_b, g_post3, tile)
        for lst, val in zip(rest_p, (small, ssm_fin, xbc[:, seq - 3:, :])):
            lst.append(val)

        (_, gda, z, xbc, small, _, gfx, qdat, kda, vda, _, qfxt, kfx, vfx, _) = _in_proj(
            xs, mod_s, l, *proj_params, PAGE)
        qm_da = _masked_queries(qdat[0].T.reshape(nseq, dec_len, ATT_W), 2)
        qm_fx = _masked_queries(qfxt[0].T.reshape(nseq, dec_len, ATT_W), 1)
        o_da = _decode("da", l, pt_flat, qm_da, kc_da, vc_da, kda, vda,
                       gda.reshape(nseq, dec_len, ATT_W), da_nw4, (bias_dec, bias_new, lamp), pages, lam_init)
        o_fx = _decode("fox", l, pt_flat, qm_fx, kc_fx, vc_fx, kfx, vfx,
                       gfx.reshape(nseq, dec_len, ATT_W), fx_nw4, (lf_cache, small), pages)

        def pad_seq(a):
            a = a.reshape(nseq, dec_len, a.shape[-1])
            return jnp.pad(a, ((0, 0), (0, chunk - dec_len), (0, 0)))

        o_ssm, ssm_fin = _ssd(l, pad_seq(xbc), pad_seq(z), pad_seq(small), conv_init_s, ssm_init_s,
                              *ssd_params, chunk, dec_len)
        o_ssm = o_ssm[:, :dec_len].reshape(1, nseq * dec_len, SSM_W)
        xs = _out_proj(l, o_da.reshape(1, nseq * dec_len, ATT_W), o_ssm, o_fx.reshape(1, nseq * dec_len, ATT_W),
                       xs, mod_s, w_out_b, g_post3, PAGE)
        for lst, val in zip(outs_s, (kda, vda, kfx, vfx, small, ssm_fin, xbc)):
            lst.append(val)

    def heads(a, n, s):
        return jnp.transpose(a.reshape(depth, n, N_HEADS, HEAD_W, s), (0, 1, 4, 2, 3))

    small_p, ssm_p, conv_p = (jnp.stack(v) for v in rest_p)
    out_p = tuple(heads(a, bsz, seq) for a in stacks) + (
        small_p[..., 8:12], ssm_p.reshape(depth, bsz, SSM_HEADS, HEAD_W, SSM_STATE), conv_p)
    kda_s, vda_s, kfx_s, vfx_s, small_s, ssm_s, xbc_s = (jnp.stack(v) for v in outs_s)
    ntok = nseq * dec_len
    out_s = tuple(heads(a, 1, ntok).reshape(depth, nseq, dec_len, N_HEADS, HEAD_W)
                  for a in (kda_s, vda_s, kfx_s, vfx_s)) + (
        small_s.reshape(depth, nseq, dec_len, LANES)[..., 8:12],
        ssm_s.reshape(depth, nseq, SSM_HEADS, HEAD_W, SSM_STATE),
        xbc_s.reshape(depth, nseq, dec_len, CONV_CH)[:, :, dec_len - 3:, :])
    return (xp, xs.reshape(nseq, dec_len, D_MODEL)) + out_p + out_s
```

```python
import functools
import math

import jax
import jax.numpy as jnp
from jax import lax
from jax.experimental import pallas as pl
from jax.experimental.pallas import tpu as pltpu

F32 = jnp.float32
BF16 = jnp.bfloat16
I32 = jnp.int32

EPS = 1e-6
NEG_INF = -1e30

D_MODEL = 1024
N_HEADS = 4
HEAD_W = 64
ATT_W = N_HEADS * HEAD_W
DA_QK = 32
SSM_HEADS = 8
SSM_W = 512
SSM_STATE = 128
CONV_W = 4
CONV_CH = 1024
N_BUCKETS = 32
MAX_DISTANCE = 128
PAGE = 128
LANES = 128
SUBLANES = 8
VMEM_LIMIT = 56 * 1024 * 1024

DA_SCALE = DA_QK ** -0.5
FOX_SCALE = HEAD_W ** -0.5
LOG2E = math.log2(math.e)

_SIZES = (256, 256, 256, 256, 512, 1024, 8, 256, 256, 256, 256, 4)
_OFFS = [0]
for _s in _SIZES:
    _OFFS.append(_OFFS[-1] + _s)
(C_DAQ, C_DAK, C_DAV, C_DAG, C_Z, C_XBC, C_DT, C_FXQ, C_FXK, C_FXV, C_FXG, C_FXF) = _OFFS[:12]


def _cp(sem):
    return pltpu.CompilerParams(dimension_semantics=sem, vmem_limit_bytes=VMEM_LIMIT)


def _silu(v):
    return v * (1.0 / (1.0 + jnp.exp(-v)))


def _split3(v):
    a = v.astype(BF16)
    r = v - a.astype(F32)
    b = r.astype(BF16)
    r = r - b.astype(F32)
    return a, b, r.astype(BF16)


def _dot(a, b):
    return jnp.dot(a, b, preferred_element_type=F32)


def _dot_nt(a, b):
    return lax.dot_general(a, b, (((1,), (1,)), ((), ())), preferred_element_type=F32)


def _dot_exact_rhs(v, w):
    a, b, c = _split3(v)
    return _dot(a, w) + _dot(b, w) + _dot(c, w)


def _dot_exact_lhs(w, v):
    a, b, c = _split3(v)
    return _dot(w, a) + _dot(w, b) + _dot(w, c)


def _adaln_kernel(c_ref, w_ref, b_ref, o_ref):
    c = c_ref[...]
    o_ref[...] = _dot(_silu(c), w_ref[...]) + b_ref[...]


def _adaln(c_all, w_ada, b_ada):
    depth = w_ada.shape[0]
    rows = c_all.shape[0]
    nblk = 3
    return pl.pallas_call(
        _adaln_kernel,
        grid=(depth, nblk),
        in_specs=[
            pl.BlockSpec((rows, D_MODEL), lambda l, j: (0, 0)),
            pl.BlockSpec((None, D_MODEL, D_MODEL), lambda l, j: (l, 0, j)),
            pl.BlockSpec((None, 1, D_MODEL), lambda l, j: (l, 0, j)),
        ],
        out_specs=pl.BlockSpec((None, rows, D_MODEL), lambda l, j: (l, 0, j)),
        out_shape=jax.ShapeDtypeStruct((depth, rows, 3 * D_MODEL), F32),
        compiler_params=_cp(("parallel", "parallel")),
        name="adaln",
    )(c_all, w_ada, b_ada.reshape(depth, 1, 3 * D_MODEL))


def _t5_kernel(tab_ref, rel_ref, o_ref):
    rel = rel_ref[...]
    n = jnp.maximum(rel, 0)
    max_exact = N_BUCKETS // 2
    nf = jnp.maximum(n, 1).astype(F32)
    large = max_exact + (jnp.log(nf / max_exact) / math.log(MAX_DISTANCE / max_exact)
                         * (N_BUCKETS - max_exact)).astype(I32)
    large = jnp.minimum(large, N_BUCKETS - 1)
    bucket = jnp.where(n < max_exact, n, large)
    for h in range(N_HEADS):
        acc = jnp.zeros(rel.shape, F32)
        for k in range(N_BUCKETS):
            acc = jnp.where(bucket == k, tab_ref[k, h], acc)
        o_ref[h] = jnp.where(rel >= 0, acc, NEG_INF)


def _t5_bias(rel_bias, rel, block_rows):
    rows, cols = rel.shape
    return pl.pallas_call(
        _t5_kernel,
        grid=(rows // block_rows,),
        in_specs=[
            pl.BlockSpec(memory_space=pltpu.SMEM),
            pl.BlockSpec((block_rows, cols), lambda i: (i, 0)),
        ],
        out_specs=pl.BlockSpec((N_HEADS, block_rows, cols), lambda i: (0, i, 0)),
        out_shape=jax.ShapeDtypeStruct((N_HEADS, rows, cols), F32),
        compiler_params=_cp(("parallel",)),
        name="t5_bias",
    )(rel_bias, rel)


def _in_proj_kernel(x_ref, shift_ref, scale_ref, g_ref, wr_ref, wt_ref, ws_ref, sb_ref, *refs):
    (kda_ref, gda_ref, z_ref, xbc_ref, small_ref, kfx_ref, gfx_ref,
     qdat_ref, kdat_ref, vdat_ref, vdatb_ref, qfxt_ref, kfxt_ref, vfxt_ref, vfxtb_ref) = refs[-15:]
    x = x_ref[...]
    y = x * lax.rsqrt(jnp.mean(x * x, axis=-1, keepdims=True) + EPS)
    h = (y * g_ref[...]) * (1.0 + scale_ref[...]) + shift_ref[...]
    hb = h.astype(BF16)

    def rows(a, b):
        return _dot(hb, wr_ref[:, a:b])

    gda_ref[...] = rows(0, 256)
    z_ref[...] = rows(256, 768)
    xbc_ref[...] = rows(768, 1792)
    gfx_ref[...] = rows(1792, 2048)

    v = _dot(hb, ws_ref[...]) + sb_ref[...]
    t = jnp.log1p(jnp.exp(-jnp.abs(v)))
    lane = lax.broadcasted_iota(I32, v.shape, 1)
    small_ref[...] = jnp.where(lane < 8, jnp.maximum(v, 0.0) + t,
                               jnp.where(lane < 12, jnp.minimum(v, 0.0) - t, 0.0))

    def cols(j):
        return _dot_nt(wt_ref[j * ATT_W:(j + 1) * ATT_W, :], hb)

    qdat_ref[...] = (cols(0) * (DA_SCALE * LOG2E)).astype(BF16)
    k = cols(1)
    kdat_ref[...] = k
    kda_ref[...] = k.T.astype(BF16)
    v = cols(2)
    vdat_ref[...] = v
    vdatb_ref[...] = v.astype(BF16)
    qfxt_ref[...] = (cols(3) * (FOX_SCALE * LOG2E)).astype(BF16)
    k = cols(4)
    kfxt_ref[...] = k
    kfx_ref[...] = k.T.astype(BF16)
    v = cols(5)
    vfxt_ref[...] = v
    vfxtb_ref[...] = v.astype(BF16)


def _layer_spec(arr, layer, grid_rank):
    rest = arr.shape[1:]
    return pl.BlockSpec((None,) + rest, lambda *g: (layer,) + (0,) * len(rest))


def _mod_spec(mod, layer, which, tm):
    if mod.shape[3] == 1:
        return pl.BlockSpec((None, None, None, 1, D_MODEL), lambda b, i: (layer, which, b, 0, 0))
    return pl.BlockSpec((None, None, None, tm, D_MODEL), lambda b, i: (layer, which, b, i, 0))


def _in_proj(x, mod, layer, g_pre, w_rows, w_kvt, w_small, small_bias, tm, stacked=None):
    bsz, seq, _ = x.shape
    nt = seq // tm

    def row_spec(width):
        return pl.BlockSpec((None, tm, width), lambda b, i: (b, i, 0))

    def row_shape(width, dt=F32):
        return jax.ShapeDtypeStruct((bsz, seq, width), dt)

    if stacked is None:
        t_spec = pl.BlockSpec((None, ATT_W, tm), lambda b, i: (b, 0, i))
        t_shape = jax.ShapeDtypeStruct((bsz, ATT_W, seq), F32)
        extra_in, extra_specs, aliases = [], [], {}
    else:
        t_spec = pl.BlockSpec((None, None, ATT_W, tm), lambda b, i: (layer, b, 0, i))
        t_shape = jax.ShapeDtypeStruct(stacked[0].shape, F32)
        extra_in = list(stacked)
        extra_specs = [pl.BlockSpec(memory_space=pl.ANY)] * 4
        aliases = {8: 8, 9: 9, 10: 12, 11: 13}
    tq_spec = pl.BlockSpec((None, ATT_W, tm), lambda b, i: (b, 0, i))
    tb_spec = pl.BlockSpec((None, None, ATT_W, tm), lambda b, i: (b, i, 0, 0))
    tq_shape = jax.ShapeDtypeStruct((bsz, ATT_W, seq), BF16)
    tb_shape = jax.ShapeDtypeStruct((bsz, nt, ATT_W, tm), BF16)
    params = (g_pre, w_rows, w_kvt, w_small, small_bias)
    return pl.pallas_call(
        _in_proj_kernel,
        grid=(bsz, nt),
        in_specs=[row_spec(D_MODEL), _mod_spec(mod, layer, 0, tm), _mod_spec(mod, layer, 1, tm)]
        + [_layer_spec(p, layer, 2) for p in params] + extra_specs,
        out_specs=[row_spec(256), row_spec(256), row_spec(512), row_spec(1024), row_spec(LANES),
                   row_spec(256), row_spec(256),
                   tq_spec, t_spec, t_spec, tb_spec, tq_spec, t_spec, t_spec, tb_spec],
        out_shape=[row_shape(256, BF16), row_shape(256), row_shape(512), row_shape(1024),
                   row_shape(LANES), row_shape(256, BF16), row_shape(256),
                   tq_shape, t_shape, t_shape, tb_shape, tq_shape, t_shape, t_shape, tb_shape],
        input_output_aliases=aliases,
        compiler_params=_cp(("parallel", "parallel")),
        name="in_proj",
    )(x, mod, mod, *params, *extra_in)


def _cumsum_kernel(x_ref, rep_ref, row_ref, carry_s):
    @pl.when(pl.program_id(1) == 0)
    def _():
        carry_s[...] = jnp.zeros_like(carry_s)

    x = x_ref[...]
    t = x.shape[0]
    r = lax.broadcasted_iota(I32, (t, t), 0)
    c = lax.broadcasted_iota(I32, (t, t), 1)
    tri = jnp.where(r >= c, 1.0, 0.0).astype(BF16)
    cum = _dot_exact_lhs(tri, x) + carry_s[...]
    carry_s[...] = cum[t - 1:t, :]
    sr = lax.broadcasted_iota(I32, (LANES, N_HEADS * LANES), 0)
    sc = lax.broadcasted_iota(I32, (LANES, N_HEADS * LANES), 1)
    spread = jnp.where(sr == SUBLANES + sc // LANES, 1.0, 0.0).astype(BF16)
    rep_ref[...] = _dot_exact_rhs(cum, spread) * LOG2E
    row_ref[...] = cum.T[SUBLANES:2 * SUBLANES, :] * LOG2E


def _cumsum(small, tm):
    bsz, seq, _ = small.shape
    return pl.pallas_call(
        _cumsum_kernel,
        grid=(bsz, seq // tm),
        in_specs=[pl.BlockSpec((None, tm, LANES), lambda b, i: (b, i, 0))],
        out_specs=[pl.BlockSpec((None, tm, N_HEADS * LANES), lambda b, i: (b, i, 0)),
                   pl.BlockSpec((None, None, SUBLANES, tm), lambda b, i: (b, i, 0, 0))],
        out_shape=[jax.ShapeDtypeStruct((bsz, seq, N_HEADS * LANES), F32),
                   jax.ShapeDtypeStruct((bsz, seq // tm, SUBLANES, tm), F32)],
        scratch_shapes=[pltpu.VMEM((1, LANES), F32)],
        compiler_params=_cp(("parallel", "arbitrary")),
        name="logf_cumsum",
    )(small)


def _lam(lamp_ref, lam_init):
    p = lamp_ref[...]
    s1 = jnp.sum(p[0:1] * p[1:2], axis=-1, keepdims=True)
    s2 = jnp.sum(p[2:3] * p[3:4], axis=-1, keepdims=True)
    return jnp.exp(s1) - jnp.exp(s2) + lam_init


def _head_norm(o, n_heads):
    lane = lax.broadcasted_iota(I32, o.shape, 1)
    o2 = o * o
    rs = jnp.zeros_like(o)
    for h in range(n_heads):
        m = (lane >= h * HEAD_W) & (lane < (h + 1) * HEAD_W)
        ss = jnp.sum(jnp.where(m, o2, 0.0), axis=-1, keepdims=True)
        rs = jnp.where(m, lax.rsqrt(ss / HEAD_W + EPS), rs)
    return o * rs


def _flash_kernel(*refs, mode, tile, lam_init):
    if mode == "da":
        (qt_ref, k_ref, v_ref, gate_ref, nw_ref, bias_ref, lamp_ref,
         o_ref, qm_s, m_s, mb_s, acc_s, s_s) = refs
        ncomp = 2
    else:
        (qt_ref, k_ref, v_ref, gate_ref, nw_ref, bias_ref, ckrep_ref, cqrow_ref,
         o_ref, qm_s, m_s, mb_s, acc_s, s_s, cq_s) = refs
        ncomp = 1
    hp = pl.program_id(1)
    qi = pl.program_id(2)
    nr = 2 * ncomp
    width = HEAD_W // ncomp
    row = lax.broadcasted_iota(I32, (LANES, tile), 0)

    qt = qt_ref[...]
    for r in range(nr):
        sel = (row >= r * width) & (row < (r + 1) * width)
        qm_s[r] = jnp.where(sel, qt, jnp.zeros_like(qt))
        m_s[r] = jnp.full((1, tile), NEG_INF, F32)
        acc_s[r] = jnp.zeros((LANES, tile), F32)
    if mode == "fox":
        for hh in range(2):
            cq_s[hh] = cqrow_ref[pl.ds(hp * 2 + hh, 1), :]

    def stage1(kj, slot):
        ks = pl.multiple_of(kj * tile, tile)
        kt = k_ref[pl.ds(ks, tile), :]
        kind = jnp.clip(kj - qi + 2, 0, 2)
        for r in range(nr):
            hh = r // ncomp
            s = _dot(kt, qm_s[r])
            if mode == "da":
                s = s + bias_ref[hh, kind]
                shift = 0.0
            else:
                ck = ckrep_ref[pl.ds(ks, tile), hh * LANES:(hh + 1) * LANES]
                s = s + bias_ref[0, kind] - jnp.tile(ck, (1, tile // LANES))
                shift = cq_s[hh]
            m_prev = m_s[r]
            m_new = jnp.maximum(m_prev, jnp.max(s, axis=0, keepdims=True) + shift)
            s_s[slot, r] = s
            mb_s[r] = m_prev
            m_s[r] = m_new

    def stage2_inputs():
        return [(mb_s[r], m_s[r]) for r in range(nr)]

    def stage2(kj, slot, pend):
        vt = v_ref[kj]
        for r in range(nr):
            hh = r // ncomp
            m_prev, m_new = pend[r]
            shift = cq_s[hh] if mode == "fox" else 0.0
            alpha = jnp.exp2(m_prev - m_new)
            p = jnp.exp2(s_s[slot, r] - (m_new - shift)).astype(BF16)
            own = (row >= hh * HEAD_W) & (row < (hh + 1) * HEAD_W)
            vmod = jnp.where(own, vt, jnp.ones_like(vt))
            acc_s[r] = alpha * acc_s[r] + _dot(vmod, p)

    def body(kj, carry):
        for par in range(2):
            @pl.when(kj % 2 == par)
            def _():
                pend = stage2_inputs()
                stage1(kj, par)
                stage2(kj - 1, 1 - par, pend)
        return carry

    stage1(0, 0)
    lax.fori_loop(1, qi + 1, body, 0)
    for par in range(2):
        @pl.when(qi % 2 == par)
        def _():
            stage2(qi, par, stage2_inputs())

    outs = []
    for hh in range(2):
        den = HEAD_W * (1 - hh)
        if mode == "da":
            a0 = acc_s[2 * hh]
            a1 = acc_s[2 * hh + 1]
            outs.append(a0 * (1.0 / a0[den:den + 1, :])
                        - a1 * (_lam(lamp_ref, lam_init) / a1[den:den + 1, :]))
        else:
            a0 = acc_s[hh]
            outs.append(a0 * (1.0 / a0[den:den + 1, :]))
    o = jnp.where(row < HEAD_W, outs[0], outs[1]).T
    y = _head_norm(o, 2) * nw_ref[...]
    if mode == "da":
        y = y * (1.0 - lam_init)
    o_ref[...] = (y * _silu(gate_ref[...])).astype(o_ref.dtype)


def _flash(mode, layer, qt, k, vt, gate, norm_w, bias, extra, tile, lam_init=0.0):
    bsz, seq, _ = k.shape
    nq = seq // tile
    ncomp = 2 if mode == "da" else 1
    nr = 2 * ncomp
    blk = pl.BlockSpec((None, tile, LANES), lambda b, p, i: (b, i, p))
    qt_spec = pl.BlockSpec((None, LANES, tile), lambda b, p, i: (b, p, i))
    k_spec = pl.BlockSpec((None, seq, LANES), lambda b, p, i: (b, 0, p))
    v_spec = pl.BlockSpec((None, nq, LANES, tile), lambda b, p, i: (b, 0, p, 0))
    in_specs = [qt_spec, k_spec, v_spec, blk, _layer_spec(norm_w, layer, 3)]
    scratch = [pltpu.VMEM((nr, LANES, tile), BF16), pltpu.VMEM((nr, 1, tile), F32),
               pltpu.VMEM((nr, 1, tile), F32), pltpu.VMEM((nr, LANES, tile), F32),
               pltpu.VMEM((2, nr, tile, tile), F32)]
    if mode == "da":
        in_specs += [pl.BlockSpec((2, 3, tile, tile), lambda b, p, i: (p, 0, 0, 0)),
                     _layer_spec(extra[0], layer, 3)]
    else:
        in_specs += [pl.BlockSpec((1, 3, tile, tile), lambda b, p, i: (0, 0, 0, 0)),
                     pl.BlockSpec((None, seq, 2 * LANES), lambda b, p, i: (b, 0, p)),
                     pl.BlockSpec((None, None, SUBLANES, tile), lambda b, p, i: (b, i, 0, 0))]
        scratch.append(pltpu.VMEM((2, 1, tile), F32))
    return pl.pallas_call(
        functools.partial(_flash_kernel, mode=mode, tile=tile, lam_init=lam_init),
        grid=(bsz, 2, nq),
        in_specs=in_specs,
        out_specs=blk,
        out_shape=jax.ShapeDtypeStruct((bsz, seq, ATT_W), BF16),
        scratch_shapes=scratch,
        compiler_params=_cp(("parallel", "parallel", "arbitrary")),
        name="flash_" + mode,
    )(qt, k, vt, gate, norm_w, bias, *extra)


def _ssd_kernel(xbc_ref, z_ref, small_ref, cinit_ref, sinit_ref, cw_ref, cb_ref, alog_ref,
                dskip_ref, nw_ref, o_ref, sfin_ref, state_s, tail_s, *, chunk, valid_len):
    c = pl.program_id(1)
    nchunks = pl.num_programs(1)

    @pl.when(c == 0)
    def _():
        state_s[...] = sinit_ref[...]
        tail_s[...] = cinit_ref[...]

    x = xbc_ref[...]
    xcat = jnp.concatenate([tail_s[...], x], axis=0)
    conv = cb_ref[...] + xcat[SUBLANES - 3:SUBLANES - 3 + chunk] * cw_ref[0:1, :]
    for j in range(1, CONV_W):
        conv = conv + xcat[SUBLANES - 3 + j:SUBLANES - 3 + j + chunk] * cw_ref[j:j + 1, :]
    tail_s[...] = x[chunk - SUBLANES:chunk]
    xc = _silu(conv)
    xs = xc[:, :SSM_W]
    bm = xc[:, SSM_W:SSM_W + 2 * SSM_STATE].astype(BF16)
    cm = xc[:, SSM_W + 2 * SSM_STATE:].astype(BF16)

    lane = lax.broadcasted_iota(I32, (chunk, LANES), 1)
    row = lax.broadcasted_iota(I32, (chunk, LANES), 0)
    dtv = jnp.where(lane < SSM_HEADS, small_ref[...], 0.0)
    if valid_len is not None:
        dtv = jnp.where(row + c * chunk < valid_len, dtv, 0.0)
    d_a = dtv * (-jnp.exp(alog_ref[...]))

    ri = lax.broadcasted_iota(I32, (chunk, chunk), 0)
    ci = lax.broadcasted_iota(I32, (chunk, chunk), 1)
    causal = ri >= ci
    tri = jnp.where(causal, 1.0, 0.0).astype(BF16)
    a_cs = _dot_exact_lhs(tri, d_a)
    a_cs_t = a_cs.T

    er = lax.broadcasted_iota(I32, (LANES, SSM_W), 0)
    ec = lax.broadcasted_iota(I32, (LANES, SSM_W), 1)
    expand = jnp.where((ec >= er * HEAD_W) & (ec < (er + 1) * HEAD_W), 1.0, 0.0).astype(BF16)
    a_exp = _dot_exact_rhs(a_cs, expand)
    dt_exp = _dot_exact_rhs(dtv, expand)
    tot_exp = a_exp[chunk - 1:chunk, :]
    e_col = jnp.exp(a_exp)
    xdt = xs * dt_exp
    xw = xdt * jnp.exp(tot_exp - a_exp)

    srow = lax.broadcasted_iota(I32, (LANES, 1), 0)
    ys = []
    for g in range(2):
        bg = bm[:, g * SSM_STATE:(g + 1) * SSM_STATE]
        cg = cm[:, g * SSM_STATE:(g + 1) * SSM_STATE]
        scores = _dot_nt(cg, bg)
        for pp in range(2):
            p = g * 2 + pp
            sl = slice(p * LANES, (p + 1) * LANES)
            xdt_p = xdt[:, sl].astype(BF16)
            yd = []
            for h in (2 * p, 2 * p + 1):
                seg = a_cs[:, h:h + 1] - a_cs_t[h:h + 1, :]
                decay = jnp.exp(jnp.where(causal, seg, NEG_INF))
                yd.append(_dot((scores * decay).astype(BF16), xdt_p))
            y_diag = jnp.where(lane < HEAD_W, yd[0], yd[1])
            prev = state_s[p]
            y_off = _dot_nt(cg, prev.astype(BF16)) * e_col[:, sl]
            cs = _dot(xw[:, sl].T.astype(BF16), bg)
            e_tot = jnp.where(srow < HEAD_W, jnp.exp(a_cs[chunk - 1:chunk, 2 * p:2 * p + 1]),
                              jnp.exp(a_cs[chunk - 1:chunk, 2 * p + 1:2 * p + 2]))
            state_s[p] = prev * e_tot + cs
            ys.append(y_diag + y_off + dskip_ref[:, sl] * xs[:, sl])
    y = jnp.concatenate(ys, axis=1)
    yz = y * _silu(z_ref[...])
    o_ref[...] = (yz * lax.rsqrt(jnp.mean(yz * yz, axis=-1, keepdims=True) + EPS) * nw_ref[...]).astype(o_ref.dtype)

    @pl.when(c == nchunks - 1)
    def _():
        sfin_ref[...] = state_s[...]


def _ssd(layer, xbc, z, small, conv_init, ssm_init, conv_w, conv_b, a_log, d_skip, norm_w, chunk, valid_len):
    bsz, seq, _ = xbc.shape

    def row_spec(width):
        return pl.BlockSpec((None, chunk, width), lambda b, c: (b, c, 0))

    def state_spec(arr, shape):
        nd = len(shape)
        if arr.ndim == nd + 2:
            return pl.BlockSpec((None, None) + shape, lambda b, c: (layer, b) + (0,) * nd)
        return pl.BlockSpec((None,) + shape, lambda b, c: (b,) + (0,) * nd)

    st_spec = pl.BlockSpec((None, 4, LANES, SSM_STATE), lambda b, c: (b, 0, 0, 0))
    params = (conv_w, conv_b, a_log, d_skip, norm_w)
    return pl.pallas_call(
        functools.partial(_ssd_kernel, chunk=chunk, valid_len=valid_len),
        grid=(bsz, seq // chunk),
        in_specs=[row_spec(CONV_CH), row_spec(SSM_W), row_spec(LANES),
                  state_spec(conv_init, (SUBLANES, CONV_CH)), state_spec(ssm_init, (4, LANES, SSM_STATE))]
        + [_layer_spec(p, layer, 2) for p in params],
        out_specs=[row_spec(SSM_W), st_spec],
        out_shape=[jax.ShapeDtypeStruct((bsz, seq, SSM_W), BF16),
                   jax.ShapeDtypeStruct((bsz, 4, LANES, SSM_STATE), F32)],
        scratch_shapes=[pltpu.VMEM((4, LANES, SSM_STATE), F32), pltpu.VMEM((SUBLANES, CONV_CH), F32)],
        compiler_params=_cp(("parallel", "arbitrary")),
        name="ssd",
    )(xbc, z, small, conv_init, ssm_init, conv_w, conv_b, a_log, d_skip, norm_w)


def _out_proj_kernel(oda_ref, ossm_ref, ofx_ref, x_ref, gate_ref, w_ref, g_ref, y_ref):
    mix = (_dot(oda_ref[...].astype(BF16), w_ref[0:ATT_W, :])
           + _dot(ossm_ref[...].astype(BF16), w_ref[ATT_W:ATT_W + SSM_W, :])
           + _dot(ofx_ref[...].astype(BF16), w_ref[ATT_W + SSM_W:, :]))
    n = mix * lax.rsqrt(jnp.mean(mix * mix, axis=-1, keepdims=True) + EPS) * g_ref[...]
    y_ref[...] = x_ref[...] + gate_ref[...] * n


def _out_proj(layer, o_da, o_ssm, o_fx, x, mod, w_out, g_post, tm):
    bsz, seq, _ = x.shape

    def row_spec(width):
        return pl.BlockSpec((None, tm, width), lambda b, i: (b, i, 0))

    return pl.pallas_call(
        _out_proj_kernel,
        grid=(bsz, seq // tm),
        in_specs=[row_spec(ATT_W), row_spec(SSM_W), row_spec(ATT_W), row_spec(D_MODEL),
                  _mod_spec(mod, layer, 2, tm), _layer_spec(w_out, layer, 2), _layer_spec(g_post, layer, 2)],
        out_specs=row_spec(D_MODEL),
        out_shape=jax.ShapeDtypeStruct((bsz, seq, D_MODEL), F32),
        compiler_params=_cp(("parallel", "parallel")),
        name="out_proj",
    )(o_da, o_ssm, o_fx, x, mod, w_out, g_post)


def _decode_kernel(pt_ref, *refs, mode, layer, pages, ng, lam_init):
    it = iter(refs)
    qm_ref = next(it)
    caches = [next(it), next(it)]
    if mode == "fox":
        caches.append(next(it))
        small_ref = next(it)
    else:
        bias_ref = next(it)
        biasn_ref = next(it)
        lamp_ref = next(it)
    knew_ref = next(it)
    vnew_ref = next(it)
    gate_ref = next(it)
    nw_ref = next(it)
    o_ref = next(it)
    bufs = [next(it) for _ in caches]
    sem = next(it)
    m_s = next(it)
    l_s = next(it)
    acc_s = next(it)
    carry_s = next(it) if mode == "fox" else None

    step = pl.program_id(0)
    nsteps = pl.num_programs(0)
    s_idx = step // ng
    g = step % ng
    slot = step % 2
    nrow = qm_ref.shape[0]

    def page_copy(kind, t, sl, j):
        page = pt_ref[t * pages + j]
        return pltpu.make_async_copy(caches[kind].at[layer, page], bufs[kind].at[sl, j], sem.at[kind, sl])

    def start_fetch(t, sl):
        for kind in range(len(caches)):
            for j in range(pages):
                page_copy(kind, t, sl, j).start()

    @pl.when(step == 0)
    def _():
        start_fetch(step, slot)

    @pl.when(step + 1 < nsteps)
    def _():
        start_fetch(step + 1, 1 - slot)

    for kind in range(len(caches)):
        for j in range(pages):
            page_copy(kind, step, slot, j).wait()

    @pl.when(g == 0)
    def _():
        m_s[...] = jnp.full(m_s.shape, NEG_INF, F32)
        l_s[...] = jnp.zeros(l_s.shape, F32)
        acc_s[...] = jnp.zeros(acc_s.shape, F32)
        if mode == "fox":
            carry_s[...] = jnp.zeros(carry_s.shape, F32)

    qm = qm_ref[...]

    def update(s, v_list):
        m_prev = m_s[...]
        m_new = jnp.maximum(m_prev, jnp.max(s, axis=-1, keepdims=True))
        alpha = jnp.exp2(m_prev - m_new)
        p = jnp.exp2(s - m_new)
        l_s[...] = alpha * l_s[...] + jnp.sum(p, axis=-1, keepdims=True)
        pv = None
        for j, load_v in enumerate(v_list):
            t = _dot_nt(p[:, j * PAGE:(j + 1) * PAGE].astype(BF16), load_v().astype(BF16))
            pv = t if pv is None else pv + t
        acc_s[...] = alpha * acc_s[...] + pv
        m_s[...] = m_new

    def expand_rows(ck):
        return jnp.concatenate(
            [jnp.broadcast_to(ck[h:h + 1, :], (SUBLANES, PAGE)) for h in range(N_HEADS)], axis=0)

    def upper(block):
        r = lax.broadcasted_iota(I32, (PAGE, PAGE), 0)
        c = lax.broadcasted_iota(I32, (PAGE, PAGE), 1)
        keep = r <= c
        if block:
            keep = keep & (r // block == c // block)
        return jnp.where(keep, 1.0, 0.0).astype(BF16)

    s_parts = [_dot(qm, bufs[0][slot, j].astype(BF16)) for j in range(pages)]
    if mode == "fox":
        lf = jnp.concatenate([bufs[2][slot, j] for j in range(pages)], axis=0)
        cum = _dot_exact_rhs(lf, upper(0))
        carry = carry_s[...]
        for j in range(pages):
            cj = cum[j * SUBLANES:(j + 1) * SUBLANES]
            s_parts[j] = s_parts[j] - expand_rows((cj + carry) * LOG2E)
            carry = carry + cj[:, PAGE - 1:PAGE]
        carry_s[...] = carry
        s = jnp.concatenate(s_parts, axis=1)
    else:
        s = jnp.concatenate(s_parts, axis=1) + bias_ref[g]
    update(s, [functools.partial(lambda j: bufs[1][slot, j], j) for j in range(pages)])

    @pl.when(g == ng - 1)
    def _():
        sn = _dot(qm, knew_ref[...].astype(BF16))
        col = lax.broadcasted_iota(I32, (nrow, PAGE), 1)
        t_row = lax.broadcasted_iota(I32, (nrow, PAGE), 0) % SUBLANES
        rel = t_row - (col - s_idx * 4)
        ok = (rel >= 0) & (col >= s_idx * 4) & (t_row < 4)
        if mode == "fox":
            lfn = small_ref[...].T[SUBLANES:2 * SUBLANES, :]
            ckn = expand_rows((_dot_exact_rhs(lfn, upper(4)) + carry_s[...]) * LOG2E)
            cq = jnp.sum(jnp.where(rel == 0, ckn, 0.0), axis=-1, keepdims=True)
            m_s[...] = m_s[...] + cq
            sn = sn - ckn + cq
        else:
            bn = biasn_ref[...]
            add = jnp.zeros_like(sn)
            for r in range(4):
                add = jnp.where(rel == r, bn[:, r:r + 1], add)
            sn = sn + add
        sn = jnp.where(ok, sn, NEG_INF)
        update(sn, [lambda: vnew_ref[...]])

        o1 = acc_s[...] / l_s[...]
        if mode == "da":
            a = o1[0:4 * SUBLANES] - _lam(lamp_ref, lam_init) * o1[4 * SUBLANES:]
        else:
            a = o1
        lane = lax.broadcasted_iota(I32, (SUBLANES, ATT_W), 1)
        o8 = jnp.zeros((SUBLANES, ATT_W), F32)
        for h in range(N_HEADS):
            o8 = jnp.where((lane >= h * HEAD_W) & (lane < (h + 1) * HEAD_W),
                           a[h * SUBLANES:(h + 1) * SUBLANES], o8)
        y = _head_norm(o8, N_HEADS) * nw_ref[...]
        if mode == "da":
            y = y * (1.0 - lam_init)
        o_ref[...] = y[0:4] * _silu(gate_ref[...])


def _decode(mode, layer, page_table, qm, k_cache, v_cache, knew, vnew, gate, norm_w, extra,
            pages, lam_init=0.0):
    nseq, nrow, _ = qm.shape
    npages = page_table.shape[0] // nseq
    ng = npages // pages

    def const(shape):
        return pl.BlockSpec(shape, lambda t, pt: (0,) * len(shape))

    def first(shape):
        return pl.BlockSpec((None,) + shape, lambda t, pt: (0,) * (len(shape) + 1))

    def per_seq(rows):
        return pl.BlockSpec((None, rows, ATT_W), lambda t, pt: (t // ng, 0, 0))

    hbm = pl.BlockSpec(memory_space=pl.ANY)
    in_specs = [per_seq(nrow), hbm, hbm]
    args = [qm, k_cache, v_cache]
    bufs = [pltpu.VMEM((2, pages, ATT_W, PAGE), F32), pltpu.VMEM((2, pages, ATT_W, PAGE), F32)]
    state = [pltpu.VMEM((nrow, 1), F32), pltpu.VMEM((nrow, 1), F32), pltpu.VMEM((nrow, ATT_W), F32)]
    if mode == "fox":
        lf_cache, small = extra
        in_specs += [hbm, first((PAGE, LANES))]
        args += [lf_cache, small]
        bufs.append(pltpu.VMEM((2, pages, SUBLANES, PAGE), F32))
        state.append(pltpu.VMEM((SUBLANES, PAGE), F32))
    else:
        bias, bias_new, lamp = extra
        in_specs += [const(bias.shape), const(bias_new.shape), _layer_spec(lamp, layer, 1)]
        args += [bias, bias_new, lamp]
    in_specs += [first((ATT_W, PAGE)), first((ATT_W, PAGE)), per_seq(4), _layer_spec(norm_w, layer, 1)]
    args += [knew, vnew, gate, norm_w]
    scratch = bufs + [pltpu.SemaphoreType.DMA((len(bufs), 2))] + state
    return pl.pallas_call(
        functools.partial(_decode_kernel, mode=mode, layer=layer, pages=pages, ng=ng, lam_init=lam_init),
        grid_spec=pltpu.PrefetchScalarGridSpec(
            num_scalar_prefetch=1, grid=(nseq * ng,), in_specs=in_specs,
            out_specs=per_seq(4), scratch_shapes=scratch),
        out_shape=jax.ShapeDtypeStruct((nseq, 4, ATT_W), F32),
        compiler_params=_cp(("arbitrary",)),
        name="decode_" + mode,
    )(page_table, *args)


def _pad_lanes(v, width=LANES):
    return jnp.pad(v, ((0, 0), (0, width - v.shape[-1])))


def _masked_queries(q, ncomp):
    nseq = q.shape[0]
    width = HEAD_W // ncomp
    lane = jnp.arange(ATT_W)
    q8 = jnp.pad(q, ((0, 0), (0, SUBLANES - 4), (0, 0)))
    blocks = []
    for c in range(ncomp):
        for h in range(N_HEADS):
            lo = h * HEAD_W + c * width
            blocks.append(jnp.where((lane >= lo) & (lane < lo + width), q8, jnp.zeros_like(q8)))
    return jnp.concatenate(blocks, axis=1).reshape(nseq, ncomp * N_HEADS * SUBLANES, ATT_W)


def kernel(x_prompt, x_sample, c_prompt, c_sample, cache_da_k, cache_da_v, cache_fox_k, cache_fox_v,
           cache_fox_logf, state_ssm, state_conv, page_table, rel_bias, w_ada, b_ada, g_pre, g_post,
           w_in, w_out, da_lam_q1, da_lam_k1, da_lam_q2, da_lam_k2, da_subln, conv_w, conv_b, dt_bias,
           a_log, d_skip, ssm_norm, fox_fbias, fox_norm):
    depth = w_in.shape[0]
    bsz, seq, _ = x_prompt.shape
    nseq, dec_len, _ = x_sample.shape
    nphys = cache_da_k.shape[1]
    npages = page_table.shape[1]
    past = npages * PAGE
    assert dec_len == 4 and nseq * dec_len == PAGE
    tile = 512 if seq % 512 == 0 else 128
    chunk = 128
    chunk_s = 16
    pages = 32 if npages % 32 == 0 else npages

    def cols(a, b):
        return w_in[:, :, a:b]

    w_rows = jnp.concatenate([cols(C_DAG, C_Z), cols(C_Z, C_XBC), cols(C_XBC, C_DT),
                              cols(C_FXG, C_FXF)], axis=-1).astype(BF16)
    w_kvt = jnp.swapaxes(jnp.concatenate([cols(C_DAQ, C_DAG), cols(C_FXQ, C_FXG)], axis=-1), 1, 2).astype(BF16)
    w_small = jnp.pad(jnp.concatenate([cols(C_DT, C_FXQ), cols(C_FXF, C_FXF + 4)], axis=-1),
                      ((0, 0), (0, 0), (0, LANES - 12))).astype(BF16)
    small_bias = _pad_lanes(jnp.concatenate([dt_bias, fox_fbias], axis=-1))
    w_out_b = w_out.astype(BF16)
    lamp = jnp.pad(jnp.stack([da_lam_q1, da_lam_k1, da_lam_q2, da_lam_k2], axis=1),
                   ((0, 0), (0, SUBLANES - 4), (0, LANES - DA_QK)))
    conv_w8 = jnp.pad(conv_w, ((0, 0), (0, SUBLANES - CONV_W), (0, 0)))
    a_log_p = _pad_lanes(a_log)
    d_skip_e = jnp.repeat(d_skip, HEAD_W, axis=-1)
    da_nw2 = jnp.tile(da_subln, (1, 2))
    da_nw4 = jnp.tile(da_subln, (1, 4))
    fx_nw2 = jnp.tile(fox_norm, (1, 2))
    fx_nw4 = jnp.tile(fox_norm, (1, 4))

    def pages_t(c):
        return jnp.transpose(c, (0, 1, 3, 4, 2)).reshape(depth, nphys, ATT_W, PAGE)

    kc_da, vc_da, kc_fx, vc_fx = (pages_t(c) for c in (cache_da_k, cache_da_v, cache_fox_k, cache_fox_v))
    lf_cache = jnp.pad(jnp.transpose(cache_fox_logf, (0, 1, 3, 2)),
                       ((0, 0), (0, 0), (0, SUBLANES - N_HEADS), (0, 0)))
    conv_init_s = jnp.pad(state_conv, ((0, 0), (0, 0), (SUBLANES - 3, 0), (0, 0)))
    ssm_init_s = state_ssm.reshape(depth, nseq, 4, LANES, SSM_STATE)
    conv_init_p = jnp.zeros((bsz, SUBLANES, CONV_CH), F32)
    ssm_init_p = jnp.zeros((bsz, 4, LANES, SSM_STATE), F32)

    nc = bsz + nseq
    nc_pad = -(-nc // SUBLANES) * SUBLANES
    c_all = jnp.pad(jnp.concatenate([c_prompt, c_sample], axis=0), ((0, nc_pad - nc), (0, 0)))
    mod = _adaln(c_all, w_ada, b_ada)

    ii = jnp.arange(tile, dtype=I32)
    d = ii[None, :] - ii[:, None]
    rel_tiles = jnp.concatenate([d + 2 * tile, d + tile, d], axis=0)
    bias_tiles = _t5_bias(rel_bias, rel_tiles, tile).reshape(N_HEADS, 3, tile, tile) * LOG2E
    tt = jnp.arange(SUBLANES, dtype=I32)
    rel_dec = jnp.where(tt[:, None] < dec_len, past + tt[:, None] - jnp.arange(past, dtype=I32)[None, :], 1)
    bias_dec = _t5_bias(rel_bias, rel_dec, SUBLANES)
    bias_dec = jnp.tile(bias_dec.reshape(N_HEADS * SUBLANES, past), (2, 1))
    bias_dec = jnp.transpose(bias_dec.reshape(2 * N_HEADS * SUBLANES, npages // pages, pages * PAGE),
                             (1, 0, 2)) * LOG2E
    bias_new = jnp.tile(jnp.repeat(_pad_lanes(rel_bias[:4].T), SUBLANES, axis=0), (2, 1)) * LOG2E
    fox_mask = jnp.stack([jnp.zeros((tile, tile), F32), jnp.zeros((tile, tile), F32),
                          jnp.where(d >= 0, 0.0, NEG_INF).astype(F32)])[None]

    mod_p = jnp.transpose(mod[:, :bsz].reshape(depth, bsz, 3, 1, D_MODEL), (0, 2, 1, 3, 4))
    mod_s = jnp.transpose(jnp.repeat(mod[:, bsz:nc], dec_len, axis=1).reshape(depth, 1, nseq * dec_len, 3, D_MODEL),
                          (0, 3, 1, 2, 4))
    proj_params = (g_pre[:, None], w_rows, w_kvt, w_small, small_bias[:, None])
    ssd_params = (conv_w8, conv_b[:, None], a_log_p[:, None], d_skip_e[:, None], ssm_norm[:, None])
    da_nw2, da_nw4, fx_nw2, fx_nw4 = (a[:, None] for a in (da_nw2, da_nw4, fx_nw2, fx_nw4))
    g_post3 = g_post[:, None]
    pt_flat = page_table.reshape(-1)

    xp = x_prompt
    xs = x_sample.reshape(1, nseq * dec_len, D_MODEL)
    stacks = tuple(jnp.zeros((depth, bsz, ATT_W, seq), F32) for _ in range(4))
    rest_p = [[] for _ in range(3)]
    outs_s = [[] for _ in range(7)]
    for l in range(depth):
        lam_init = 0.8 - 0.6 * math.exp(-0.3 * l)

        res = _in_proj(xp, mod_p, l, *proj_params, tile, stacked=stacks)
        kdar, gda, z, xbc, small, kfxr, gfx, qdat, _, _, vdab, qfxt, _, _, vfxb = res
        stacks = (res[8], res[9], res[12], res[13])
        ckrep, cqrow = _cumsum(small, tile)
        o_da = _flash("da", l, qdat, kdar, vdab, gda, da_nw2, bias_tiles, (lamp,), tile, lam_init)
        o_fx = _flash("fox", l, qfxt, kfxr, vfxb, gfx, fx_nw2, fox_mask, (ckrep, cqrow), tile)
        o_ssm, ssm_fin = _ssd(l, xbc, z, small, conv_init_p, ssm_init_p, *ssd_params, chunk, None)
        xp = _out_proj(l, o_da, o_ssm, o_fx, xp, mod_p, w_out_b, g_post3, tile)
        for lst, val in zip(rest_p, (small, ssm_fin, xbc[:, seq - 3:, :])):
            lst.append(val)

        (_, gda, z, xbc, small, _, gfx, qdat, kda, vda, _, qfxt, kfx, vfx, _) = _in_proj(
            xs, mod_s, l, *proj_params, PAGE)
        qm_da = _masked_queries(qdat[0].T.reshape(nseq, dec_len, ATT_W), 2)
        qm_fx = _masked_queries(qfxt[0].T.reshape(nseq, dec_len, ATT_W), 1)
        o_da = _decode("da", l, pt_flat, qm_da, kc_da, vc_da, kda, vda,
                       gda.reshape(nseq, dec_len, ATT_W), da_nw4, (bias_dec, bias_new, lamp), pages, lam_init)
        o_fx = _decode("fox", l, pt_flat, qm_fx, kc_fx, vc_fx, kfx, vfx,
                       gfx.reshape(nseq, dec_len, ATT_W), fx_nw4, (lf_cache, small), pages)

        def pad_seq(a):
            a = a.reshape(nseq, dec_len, a.shape[-1])
            return jnp.pad(a, ((0, 0), (0, chunk_s - dec_len), (0, 0)))

        o_ssm, ssm_fin = _ssd(l, pad_seq(xbc), pad_seq(z), pad_seq(small), conv_init_s, ssm_init_s,
                              *ssd_params, chunk_s, dec_len)
        o_ssm = o_ssm[:, :dec_len].reshape(1, nseq * dec_len, SSM_W)
        xs = _out_proj(l, o_da.reshape(1, nseq * dec_len, ATT_W), o_ssm, o_fx.reshape(1, nseq * dec_len, ATT_W),
                       xs, mod_s, w_out_b, g_post3, PAGE)
        for lst, val in zip(outs_s, (kda, vda, kfx, vfx, small, ssm_fin, xbc)):
            lst.append(val)

    def heads(a, n, s):
        return jnp.transpose(a.reshape(depth, n, N_HEADS, HEAD_W, s), (0, 1, 4, 2, 3))

    small_p, ssm_p, conv_p = (jnp.stack(v) for v in rest_p)
    out_p = tuple(heads(a, bsz, seq) for a in stacks) + (
        small_p[..., 8:12], ssm_p.reshape(depth, bsz, SSM_HEADS, HEAD_W, SSM_STATE), conv_p)
    kda_s, vda_s, kfx_s, vfx_s, small_s, ssm_s, xbc_s = (jnp.stack(v) for v in outs_s)
    ntok = nseq * dec_len
    out_s = tuple(heads(a, 1, ntok).reshape(depth, nseq, dec_len, N_HEADS, HEAD_W)
                  for a in (kda_s, vda_s, kfx_s, vfx_s)) + (
        small_s.reshape(depth, nseq, dec_len, LANES)[..., 8:12],
        ssm_s.reshape(depth, nseq, SSM_HEADS, HEAD_W, SSM_STATE),
        xbc_s.reshape(depth, nseq, dec_len, CONV_CH)[:, :, dec_len - 3:, :])
    return (xp, xs.reshape(nseq, dec_len, D_MODEL)) + out_p + out_s
```

```python
import functools
import math

import jax
import jax.numpy as jnp
from jax import lax
from jax.experimental import pallas as pl
from jax.experimental.pallas import tpu as pltpu

F32 = jnp.float32
BF16 = jnp.bfloat16
I32 = jnp.int32

EPS = 1e-6
NEG_INF = -1e30

D_MODEL = 1024
N_HEADS = 4
HEAD_W = 64
ATT_W = N_HEADS * HEAD_W
DA_QK = 32
SSM_HEADS = 8
SSM_W = 512
SSM_STATE = 128
CONV_W = 4
CONV_CH = 1024
N_BUCKETS = 32
MAX_DISTANCE = 128
PAGE = 128
LANES = 128
SUBLANES = 8
VMEM_LIMIT = 56 * 1024 * 1024

DA_SCALE = DA_QK ** -0.5
FOX_SCALE = HEAD_W ** -0.5
LOG2E = math.log2(math.e)

_SIZES = (256, 256, 256, 256, 512, 1024, 8, 256, 256, 256, 256, 4)
_OFFS = [0]
for _s in _SIZES:
    _OFFS.append(_OFFS[-1] + _s)
(C_DAQ, C_DAK, C_DAV, C_DAG, C_Z, C_XBC, C_DT, C_FXQ, C_FXK, C_FXV, C_FXG, C_FXF) = _OFFS[:12]


def _cp(sem):
    return pltpu.CompilerParams(dimension_semantics=sem, vmem_limit_bytes=VMEM_LIMIT)


def _silu(v):
    return v * (1.0 / (1.0 + jnp.exp(-v)))


def _split3(v):
    a = v.astype(BF16)
    r = v - a.astype(F32)
    b = r.astype(BF16)
    r = r - b.astype(F32)
    return a, b, r.astype(BF16)


def _dot(a, b):
    return jnp.dot(a, b, preferred_element_type=F32)


def _dot_nt(a, b):
    return lax.dot_general(a, b, (((1,), (1,)), ((), ())), preferred_element_type=F32)


def _dot_exact_rhs(v, w):
    a, b, c = _split3(v)
    return _dot(a, w) + _dot(b, w) + _dot(c, w)


def _dot_exact_lhs(w, v):
    a, b, c = _split3(v)
    return _dot(w, a) + _dot(w, b) + _dot(w, c)


def _adaln_kernel(c_ref, w_ref, b_ref, o_ref):
    c = c_ref[...]
    o_ref[...] = _dot(_silu(c), w_ref[...]) + b_ref[...]


def _adaln(c_all, w_ada, b_ada):
    depth = w_ada.shape[0]
    rows = c_all.shape[0]
    nblk = 3
    return pl.pallas_call(
        _adaln_kernel,
        grid=(depth, nblk),
        in_specs=[
            pl.BlockSpec((rows, D_MODEL), lambda l, j: (0, 0)),
            pl.BlockSpec((None, D_MODEL, D_MODEL), lambda l, j: (l, 0, j)),
            pl.BlockSpec((None, 1, D_MODEL), lambda l, j: (l, 0, j)),
        ],
        out_specs=pl.BlockSpec((None, rows, D_MODEL), lambda l, j: (l, 0, j)),
        out_shape=jax.ShapeDtypeStruct((depth, rows, 3 * D_MODEL), F32),
        compiler_params=_cp(("parallel", "parallel")),
        name="adaln",
    )(c_all, w_ada, b_ada.reshape(depth, 1, 3 * D_MODEL))


def _t5_kernel(tab_ref, rel_ref, o_ref):
    rel = rel_ref[...]
    n = jnp.maximum(rel, 0)
    max_exact = N_BUCKETS // 2
    nf = jnp.maximum(n, 1).astype(F32)
    large = max_exact + (jnp.log(nf / max_exact) / math.log(MAX_DISTANCE / max_exact)
                         * (N_BUCKETS - max_exact)).astype(I32)
    large = jnp.minimum(large, N_BUCKETS - 1)
    bucket = jnp.where(n < max_exact, n, large)
    for h in range(N_HEADS):
        acc = jnp.zeros(rel.shape, F32)
        for k in range(N_BUCKETS):
            acc = jnp.where(bucket == k, tab_ref[k, h], acc)
        o_ref[h] = jnp.where(rel >= 0, acc, NEG_INF)


def _t5_bias(rel_bias, rel, block_rows):
    rows, cols = rel.shape
    return pl.pallas_call(
        _t5_kernel,
        grid=(rows // block_rows,),
        in_specs=[
            pl.BlockSpec(memory_space=pltpu.SMEM),
            pl.BlockSpec((block_rows, cols), lambda i: (i, 0)),
        ],
        out_specs=pl.BlockSpec((N_HEADS, block_rows, cols), lambda i: (0, i, 0)),
        out_shape=jax.ShapeDtypeStruct((N_HEADS, rows, cols), F32),
        compiler_params=_cp(("parallel",)),
        name="t5_bias",
    )(rel_bias, rel)


def _in_proj_kernel(x_ref, shift_ref, scale_ref, g_ref, wr_ref, wt_ref, ws_ref, sb_ref, *refs):
    (kda_ref, gda_ref, z_ref, xbc_ref, small_ref, kfx_ref, gfx_ref,
     qdat_ref, kdat_ref, vdat_ref, vdatb_ref, qfxt_ref, kfxt_ref, vfxt_ref, vfxtb_ref) = refs[-15:]
    x = x_ref[...]
    y = x * lax.rsqrt(jnp.mean(x * x, axis=-1, keepdims=True) + EPS)
    h = (y * g_ref[...]) * (1.0 + scale_ref[...]) + shift_ref[...]
    hb = h.astype(BF16)

    def rows(a, b):
        return _dot(hb, wr_ref[:, a:b])

    gda_ref[...] = rows(0, 256)
    z_ref[...] = rows(256, 768)
    xbc_ref[...] = rows(768, 1792)
    gfx_ref[...] = rows(1792, 2048)

    v = _dot(hb, ws_ref[...]) + sb_ref[...]
    t = jnp.log1p(jnp.exp(-jnp.abs(v)))
    lane = lax.broadcasted_iota(I32, v.shape, 1)
    small_ref[...] = jnp.where(lane < 8, jnp.maximum(v, 0.0) + t,
                               jnp.where(lane < 12, jnp.minimum(v, 0.0) - t, 0.0))

    def cols(j):
        return _dot_nt(wt_ref[j * ATT_W:(j + 1) * ATT_W, :], hb)

    qdat_ref[...] = (cols(0) * (DA_SCALE * LOG2E)).astype(BF16)
    k = cols(1)
    kdat_ref[...] = k
    kda_ref[...] = k.T.astype(BF16)
    v = cols(2)
    vdat_ref[...] = v
    vdatb_ref[...] = v.astype(BF16)
    qfxt_ref[...] = (cols(3) * (FOX_SCALE * LOG2E)).astype(BF16)
    k = cols(4)
    kfxt_ref[...] = k
    kfx_ref[...] = k.T.astype(BF16)
    v = cols(5)
    vfxt_ref[...] = v
    vfxtb_ref[...] = v.astype(BF16)


def _layer_spec(arr, layer, grid_rank):
    rest = arr.shape[1:]
    return pl.BlockSpec((None,) + rest, lambda *g: (layer,) + (0,) * len(rest))


def _mod_spec(mod, layer, which, tm):
    if mod.shape[3] == 1:
        return pl.BlockSpec((None, None, None, 1, D_MODEL), lambda b, i: (layer, which, b, 0, 0))
    return pl.BlockSpec((None, None, None, tm, D_MODEL), lambda b, i: (layer, which, b, i, 0))


def _in_proj(x, mod, layer, g_pre, w_rows, w_kvt, w_small, small_bias, tm, stacked=None):
    bsz, seq, _ = x.shape
    nt = seq // tm

    def row_spec(width):
        return pl.BlockSpec((None, tm, width), lambda b, i: (b, i, 0))

    def row_shape(width, dt=F32):
        return jax.ShapeDtypeStruct((bsz, seq, width), dt)

    if stacked is None:
        t_spec = pl.BlockSpec((None, ATT_W, tm), lambda b, i: (b, 0, i))
        t_shape = jax.ShapeDtypeStruct((bsz, ATT_W, seq), F32)
        extra_in, extra_specs, aliases = [], [], {}
    else:
        t_spec = pl.BlockSpec((None, None, ATT_W, tm), lambda b, i: (layer, b, 0, i))
        t_shape = jax.ShapeDtypeStruct(stacked[0].shape, F32)
        extra_in = list(stacked)
        extra_specs = [pl.BlockSpec(memory_space=pl.ANY)] * 4
        aliases = {8: 8, 9: 9, 10: 12, 11: 13}
    tq_spec = pl.BlockSpec((None, ATT_W, tm), lambda b, i: (b, 0, i))
    tb_spec = pl.BlockSpec((None, None, ATT_W, tm), lambda b, i: (b, i, 0, 0))
    tq_shape = jax.ShapeDtypeStruct((bsz, ATT_W, seq), BF16)
    tb_shape = jax.ShapeDtypeStruct((bsz, nt, ATT_W, tm), BF16)
    params = (g_pre, w_rows, w_kvt, w_small, small_bias)
    return pl.pallas_call(
        _in_proj_kernel,
        grid=(bsz, nt),
        in_specs=[row_spec(D_MODEL), _mod_spec(mod, layer, 0, tm), _mod_spec(mod, layer, 1, tm)]
        + [_layer_spec(p, layer, 2) for p in params] + extra_specs,
        out_specs=[row_spec(256), row_spec(256), row_spec(512), row_spec(1024), row_spec(LANES),
                   row_spec(256), row_spec(256),
                   tq_spec, t_spec, t_spec, tb_spec, tq_spec, t_spec, t_spec, tb_spec],
        out_shape=[row_shape(256, BF16), row_shape(256), row_shape(512), row_shape(1024),
                   row_shape(LANES), row_shape(256, BF16), row_shape(256),
                   tq_shape, t_shape, t_shape, tb_shape, tq_shape, t_shape, t_shape, tb_shape],
        input_output_aliases=aliases,
        compiler_params=_cp(("parallel", "parallel")),
        name="in_proj",
    )(x, mod, mod, *params, *extra_in)


def _cumsum_kernel(x_ref, rep_ref, row_ref, carry_s):
    @pl.when(pl.program_id(1) == 0)
    def _():
        carry_s[...] = jnp.zeros_like(carry_s)

    x = x_ref[...]
    t = x.shape[0]
    r = lax.broadcasted_iota(I32, (t, t), 0)
    c = lax.broadcasted_iota(I32, (t, t), 1)
    tri = jnp.where(r >= c, 1.0, 0.0).astype(BF16)
    cum = _dot_exact_lhs(tri, x) + carry_s[...]
    carry_s[...] = cum[t - 1:t, :]
    sr = lax.broadcasted_iota(I32, (LANES, N_HEADS * LANES), 0)
    sc = lax.broadcasted_iota(I32, (LANES, N_HEADS * LANES), 1)
    spread = jnp.where(sr == SUBLANES + sc // LANES, 1.0, 0.0).astype(BF16)
    rep_ref[...] = _dot_exact_rhs(cum, spread) * LOG2E
    row_ref[...] = cum.T[SUBLANES:2 * SUBLANES, :] * LOG2E


def _cumsum(small, tm):
    bsz, seq, _ = small.shape
    return pl.pallas_call(
        _cumsum_kernel,
        grid=(bsz, seq // tm),
        in_specs=[pl.BlockSpec((None, tm, LANES), lambda b, i: (b, i, 0))],
        out_specs=[pl.BlockSpec((None, tm, N_HEADS * LANES), lambda b, i: (b, i, 0)),
                   pl.BlockSpec((None, None, SUBLANES, tm), lambda b, i: (b, i, 0, 0))],
        out_shape=[jax.ShapeDtypeStruct((bsz, seq, N_HEADS * LANES), F32),
                   jax.ShapeDtypeStruct((bsz, seq // tm, SUBLANES, tm), F32)],
        scratch_shapes=[pltpu.VMEM((1, LANES), F32)],
        compiler_params=_cp(("parallel", "arbitrary")),
        name="logf_cumsum",
    )(small)


def _lam(lamp_ref, lam_init):
    p = lamp_ref[...]
    s1 = jnp.sum(p[0:1] * p[1:2], axis=-1, keepdims=True)
    s2 = jnp.sum(p[2:3] * p[3:4], axis=-1, keepdims=True)
    return jnp.exp(s1) - jnp.exp(s2) + lam_init


def _head_norm(o, n_heads):
    lane = lax.broadcasted_iota(I32, o.shape, 1)
    o2 = o * o
    rs = jnp.zeros_like(o)
    for h in range(n_heads):
        m = (lane >= h * HEAD_W) & (lane < (h + 1) * HEAD_W)
        ss = jnp.sum(jnp.where(m, o2, 0.0), axis=-1, keepdims=True)
        rs = jnp.where(m, lax.rsqrt(ss / HEAD_W + EPS), rs)
    return o * rs


def _flash_kernel(*refs, mode, tile, lam_init):
    if mode == "da":
        (qt_ref, k_ref, v_ref, gate_ref, nw_ref, bias_ref, lamp_ref,
         o_ref, qm_s, m_s, mb_s, acc_s, s_s) = refs
        ncomp = 2
    else:
        (qt_ref, k_ref, v_ref, gate_ref, nw_ref, bias_ref, ckrep_ref, cqrow_ref,
         o_ref, qm_s, m_s, mb_s, acc_s, s_s, cq_s) = refs
        ncomp = 1
    hp = pl.program_id(1)
    qi = pl.program_id(2)
    nr = 2 * ncomp
    width = HEAD_W // ncomp
    row = lax.broadcasted_iota(I32, (LANES, tile), 0)

    qt = qt_ref[...]
    for r in range(nr):
        sel = (row >= r * width) & (row < (r + 1) * width)
        qm_s[r] = jnp.where(sel, qt, jnp.zeros_like(qt))
        m_s[r] = jnp.full((1, tile), NEG_INF, F32)
        acc_s[r] = jnp.zeros((LANES, tile), F32)
    if mode == "fox":
        for hh in range(2):
            cq_s[hh] = cqrow_ref[pl.ds(hp * 2 + hh, 1), :]

    def stage1(kj, slot):
        ks = pl.multiple_of(kj * tile, tile)
        kt = k_ref[pl.ds(ks, tile), :]
        kind = jnp.clip(kj - qi + 2, 0, 2)
        for r in range(nr):
            hh = r // ncomp
            s = _dot(kt, qm_s[r])
            if mode == "da":
                s = s + bias_ref[hh, kind]
                shift = 0.0
            else:
                ck = ckrep_ref[pl.ds(ks, tile), hh * LANES:(hh + 1) * LANES]
                s = s + bias_ref[0, kind] - jnp.tile(ck, (1, tile // LANES))
                shift = cq_s[hh]
            m_prev = m_s[r]
            m_new = jnp.maximum(m_prev, jnp.max(s, axis=0, keepdims=True) + shift)
            s_s[slot, r] = s
            mb_s[r] = m_prev
            m_s[r] = m_new

    def stage2_inputs():
        return [(mb_s[r], m_s[r]) for r in range(nr)]

    def stage2(kj, slot, pend):
        vt = v_ref[kj]
        for r in range(nr):
            hh = r // ncomp
            m_prev, m_new = pend[r]
            shift = cq_s[hh] if mode == "fox" else 0.0
            alpha = jnp.exp2(m_prev - m_new)
            p = jnp.exp2(s_s[slot, r] - (m_new - shift)).astype(BF16)
            own = (row >= hh * HEAD_W) & (row < (hh + 1) * HEAD_W)
            vmod = jnp.where(own, vt, jnp.ones_like(vt))
            acc_s[r] = alpha * acc_s[r] + _dot(vmod, p)

    def body(kj, carry):
        for par in range(2):
            @pl.when(kj % 2 == par)
            def _():
                pend = stage2_inputs()
                stage1(kj, par)
                stage2(kj - 1, 1 - par, pend)
        return carry

    stage1(0, 0)
    lax.fori_loop(1, qi + 1, body, 0)
    for par in range(2):
        @pl.when(qi % 2 == par)
        def _():
            stage2(qi, par, stage2_inputs())

    outs = []
    for hh in range(2):
        den = HEAD_W * (1 - hh)
        if mode == "da":
            a0 = acc_s[2 * hh]
            a1 = acc_s[2 * hh + 1]
            outs.append(a0 * (1.0 / a0[den:den + 1, :])
                        - a1 * (_lam(lamp_ref, lam_init) / a1[den:den + 1, :]))
        else:
            a0 = acc_s[hh]
            outs.append(a0 * (1.0 / a0[den:den + 1, :]))
    o = jnp.where(row < HEAD_W, outs[0], outs[1]).T
    y = _head_norm(o, 2) * nw_ref[...]
    if mode == "da":
        y = y * (1.0 - lam_init)
    o_ref[...] = (y * _silu(gate_ref[...])).astype(o_ref.dtype)


def _flash(mode, layer, qt, k, vt, gate, norm_w, bias, extra, tile, lam_init=0.0):
    bsz, seq, _ = k.shape
    nq = seq // tile
    ncomp = 2 if mode == "da" else 1
    nr = 2 * ncomp
    blk = pl.BlockSpec((None, tile, LANES), lambda b, p, i: (b, i, p))
    qt_spec = pl.BlockSpec((None, LANES, tile), lambda b, p, i: (b, p, i))
    k_spec = pl.BlockSpec((None, seq, LANES), lambda b, p, i: (b, 0, p))
    v_spec = pl.BlockSpec((None, nq, LANES, tile), lambda b, p, i: (b, 0, p, 0))
    in_specs = [qt_spec, k_spec, v_spec, blk, _layer_spec(norm_w, layer, 3)]
    scratch = [pltpu.VMEM((nr, LANES, tile), BF16), pltpu.VMEM((nr, 1, tile), F32),
               pltpu.VMEM((nr, 1, tile), F32), pltpu.VMEM((nr, LANES, tile), F32),
               pltpu.VMEM((2, nr, tile, tile), F32)]
    if mode == "da":
        in_specs += [pl.BlockSpec((2, 3, tile, tile), lambda b, p, i: (p, 0, 0, 0)),
                     _layer_spec(extra[0], layer, 3)]
    else:
        in_specs += [pl.BlockSpec((1, 3, tile, tile), lambda b, p, i: (0, 0, 0, 0)),
                     pl.BlockSpec((None, seq, 2 * LANES), lambda b, p, i: (b, 0, p)),
                     pl.BlockSpec((None, None, SUBLANES, tile), lambda b, p, i: (b, i, 0, 0))]
        scratch.append(pltpu.VMEM((2, 1, tile), F32))
    return pl.pallas_call(
        functools.partial(_flash_kernel, mode=mode, tile=tile, lam_init=lam_init),
        grid=(bsz, 2, nq),
        in_specs=in_specs,
        out_specs=blk,
        out_shape=jax.ShapeDtypeStruct((bsz, seq, ATT_W), BF16),
        scratch_shapes=scratch,
        compiler_params=_cp(("parallel", "parallel", "arbitrary")),
        name="flash_" + mode,
    )(qt, k, vt, gate, norm_w, bias, *extra)


def _ssd_kernel(xbc_ref, z_ref, small_ref, cinit_ref, sinit_ref, cw_ref, cb_ref, alog_ref,
                dskip_ref, nw_ref, o_ref, sfin_ref, state_s, tail_s, *, chunk, valid_len):
    c = pl.program_id(1)
    nchunks = pl.num_programs(1)

    @pl.when(c == 0)
    def _():
        state_s[...] = sinit_ref[...]
        tail_s[...] = cinit_ref[...]

    x = xbc_ref[...]
    xcat = jnp.concatenate([tail_s[...], x], axis=0)
    conv = cb_ref[...] + xcat[SUBLANES - 3:SUBLANES - 3 + chunk] * cw_ref[0:1, :]
    for j in range(1, CONV_W):
        conv = conv + xcat[SUBLANES - 3 + j:SUBLANES - 3 + j + chunk] * cw_ref[j:j + 1, :]
    tail_s[...] = x[chunk - SUBLANES:chunk]
    xc = _silu(conv)
    xs = xc[:, :SSM_W]
    bm = xc[:, SSM_W:SSM_W + 2 * SSM_STATE].astype(BF16)
    cm = xc[:, SSM_W + 2 * SSM_STATE:].astype(BF16)

    lane = lax.broadcasted_iota(I32, (chunk, LANES), 1)
    row = lax.broadcasted_iota(I32, (chunk, LANES), 0)
    dtv = jnp.where(lane < SSM_HEADS, small_ref[...], 0.0)
    if valid_len is not None:
        dtv = jnp.where(row + c * chunk < valid_len, dtv, 0.0)
    d_a = dtv * (-jnp.exp(alog_ref[...]))

    ri = lax.broadcasted_iota(I32, (chunk, chunk), 0)
    ci = lax.broadcasted_iota(I32, (chunk, chunk), 1)
    causal = ri >= ci
    tri = jnp.where(causal, 1.0, 0.0).astype(BF16)
    a_cs = _dot_exact_lhs(tri, d_a)
    a_cs_t = a_cs.T

    er = lax.broadcasted_iota(I32, (LANES, SSM_W), 0)
    ec = lax.broadcasted_iota(I32, (LANES, SSM_W), 1)
    expand = jnp.where((ec >= er * HEAD_W) & (ec < (er + 1) * HEAD_W), 1.0, 0.0).astype(BF16)
    a_exp = _dot_exact_rhs(a_cs, expand)
    dt_exp = _dot_exact_rhs(dtv, expand)
    tot_exp = a_exp[chunk - 1:chunk, :]
    e_col = jnp.exp(a_exp)
    xdt = xs * dt_exp
    xw = xdt * jnp.exp(tot_exp - a_exp)

    srow = lax.broadcasted_iota(I32, (LANES, 1), 0)
    ys = []
    for g in range(2):
        bg = bm[:, g * SSM_STATE:(g + 1) * SSM_STATE]
        cg = cm[:, g * SSM_STATE:(g + 1) * SSM_STATE]
        scores = _dot_nt(cg, bg)
        for pp in range(2):
            p = g * 2 + pp
            sl = slice(p * LANES, (p + 1) * LANES)
            xdt_p = xdt[:, sl].astype(BF16)
            yd = []
            for h in (2 * p, 2 * p + 1):
                seg = a_cs[:, h:h + 1] - a_cs_t[h:h + 1, :]
                decay = jnp.exp(jnp.where(causal, seg, NEG_INF))
                yd.append(_dot((scores * decay).astype(BF16), xdt_p))
            y_diag = jnp.where(lane < HEAD_W, yd[0], yd[1])
            prev = state_s[p]
            y_off = _dot_nt(cg, prev.astype(BF16)) * e_col[:, sl]
            cs = _dot(xw[:, sl].T.astype(BF16), bg)
            e_tot = jnp.where(srow < HEAD_W, jnp.exp(a_cs[chunk - 1:chunk, 2 * p:2 * p + 1]),
                              jnp.exp(a_cs[chunk - 1:chunk, 2 * p + 1:2 * p + 2]))
            state_s[p] = prev * e_tot + cs
            ys.append(y_diag + y_off + dskip_ref[:, sl] * xs[:, sl])
    y = jnp.concatenate(ys, axis=1)
    yz = y * _silu(z_ref[...])
    o_ref[...] = (yz * lax.rsqrt(jnp.mean(yz * yz, axis=-1, keepdims=True) + EPS) * nw_ref[...]).astype(o_ref.dtype)

    @pl.when(c == nchunks - 1)
    def _():
        sfin_ref[...] = state_s[...]


def _ssd(layer, xbc, z, small, conv_init, ssm_init, conv_w, conv_b, a_log, d_skip, norm_w, chunk, valid_len):
    bsz, seq, _ = xbc.shape

    def row_spec(width):
        return pl.BlockSpec((None, chunk, width), lambda b, c: (b, c, 0))

    def state_spec(arr, shape):
        nd = len(shape)
        if arr.ndim == nd + 2:
            return pl.BlockSpec((None, None) + shape, lambda b, c: (layer, b) + (0,) * nd)
        return pl.BlockSpec((None,) + shape, lambda b, c: (b,) + (0,) * nd)

    st_spec = pl.BlockSpec((None, 4, LANES, SSM_STATE), lambda b, c: (b, 0, 0, 0))
    params = (conv_w, conv_b, a_log, d_skip, norm_w)
    return pl.pallas_call(
        functools.partial(_ssd_kernel, chunk=chunk, valid_len=valid_len),
        grid=(bsz, seq // chunk),
        in_specs=[row_spec(CONV_CH), row_spec(SSM_W), row_spec(LANES),
                  state_spec(conv_init, (SUBLANES, CONV_CH)), state_spec(ssm_init, (4, LANES, SSM_STATE))]
        + [_layer_spec(p, layer, 2) for p in params],
        out_specs=[row_spec(SSM_W), st_spec],
        out_shape=[jax.ShapeDtypeStruct((bsz, seq, SSM_W), BF16),
                   jax.ShapeDtypeStruct((bsz, 4, LANES, SSM_STATE), F32)],
        scratch_shapes=[pltpu.VMEM((4, LANES, SSM_STATE), F32), pltpu.VMEM((SUBLANES, CONV_CH), F32)],
        compiler_params=_cp(("parallel", "arbitrary")),
        name="ssd",
    )(xbc, z, small, conv_init, ssm_init, conv_w, conv_b, a_log, d_skip, norm_w)


def _out_proj_kernel(oda_ref, ossm_ref, ofx_ref, x_ref, gate_ref, w_ref, g_ref, y_ref):
    mix = (_dot(oda_ref[...].astype(BF16), w_ref[0:ATT_W, :])
           + _dot(ossm_ref[...].astype(BF16), w_ref[ATT_W:ATT_W + SSM_W, :])
           + _dot(ofx_ref[...].astype(BF16), w_ref[ATT_W + SSM_W:, :]))
    n = mix * lax.rsqrt(jnp.mean(mix * mix, axis=-1, keepdims=True) + EPS) * g_ref[...]
    y_ref[...] = x_ref[...] + gate_ref[...] * n


def _out_proj(layer, o_da, o_ssm, o_fx, x, mod, w_out, g_post, tm):
    bsz, seq, _ = x.shape

    def row_spec(width):
        return pl.BlockSpec((None, tm, width), lambda b, i: (b, i, 0))

    return pl.pallas_call(
        _out_proj_kernel,
        grid=(bsz, seq // tm),
        in_specs=[row_spec(ATT_W), row_spec(SSM_W), row_spec(ATT_W), row_spec(D_MODEL),
                  _mod_spec(mod, layer, 2, tm), _layer_spec(w_out, layer, 2), _layer_spec(g_post, layer, 2)],
        out_specs=row_spec(D_MODEL),
        out_shape=jax.ShapeDtypeStruct((bsz, seq, D_MODEL), F32),
        compiler_params=_cp(("parallel", "parallel")),
        name="out_proj",
    )(o_da, o_ssm, o_fx, x, mod, w_out, g_post)


def _decode_kernel(pt_ref, *refs, mode, layer, pages, ng, lam_init):
    it = iter(refs)
    qm_ref = next(it)
    caches = [next(it), next(it)]
    if mode == "fox":
        caches.append(next(it))
        small_ref = next(it)
    else:
        bias_ref = next(it)
        biasn_ref = next(it)
        lamp_ref = next(it)
    knew_ref = next(it)
    vnew_ref = next(it)
    gate_ref = next(it)
    nw_ref = next(it)
    o_ref = next(it)
    bufs = [next(it) for _ in caches]
    sem = next(it)
    m_s = next(it)
    l_s = next(it)
    acc_s = next(it)
    carry_s = next(it) if mode == "fox" else None

    step = pl.program_id(0)
    nsteps = pl.num_programs(0)
    s_idx = step // ng
    g = step % ng
    slot = step % 2
    nrow = qm_ref.shape[0]

    def page_copy(kind, t, sl, j):
        page = pt_ref[t * pages + j]
        return pltpu.make_async_copy(caches[kind].at[layer, page], bufs[kind].at[sl, j], sem.at[kind, sl])

    def start_fetch(t, sl):
        for kind in range(len(caches)):
            for j in range(pages):
                page_copy(kind, t, sl, j).start()

    @pl.when(step == 0)
    def _():
        start_fetch(step, slot)

    @pl.when(step + 1 < nsteps)
    def _():
        start_fetch(step + 1, 1 - slot)

    for kind in range(len(caches)):
        for j in range(pages):
            page_copy(kind, step, slot, j).wait()

    @pl.when(g == 0)
    def _():
        m_s[...] = jnp.full(m_s.shape, NEG_INF, F32)
        l_s[...] = jnp.zeros(l_s.shape, F32)
        acc_s[...] = jnp.zeros(acc_s.shape, F32)
        if mode == "fox":
            carry_s[...] = jnp.zeros(carry_s.shape, F32)

    qm = qm_ref[...]

    def update(s, v_list):
        m_prev = m_s[...]
        m_new = jnp.maximum(m_prev, jnp.max(s, axis=-1, keepdims=True))
        alpha = jnp.exp2(m_prev - m_new)
        p = jnp.exp2(s - m_new)
        l_s[...] = alpha * l_s[...] + jnp.sum(p, axis=-1, keepdims=True)
        pv = None
        for j, load_v in enumerate(v_list):
            t = _dot_nt(p[:, j * PAGE:(j + 1) * PAGE].astype(BF16), load_v().astype(BF16))
            pv = t if pv is None else pv + t
        acc_s[...] = alpha * acc_s[...] + pv
        m_s[...] = m_new

    def expand_rows(ck):
        return jnp.concatenate(
            [jnp.broadcast_to(ck[h:h + 1, :], (SUBLANES, PAGE)) for h in range(N_HEADS)], axis=0)

    def upper(block):
        r = lax.broadcasted_iota(I32, (PAGE, PAGE), 0)
        c = lax.broadcasted_iota(I32, (PAGE, PAGE), 1)
        keep = r <= c
        if block:
            keep = keep & (r // block == c // block)
        return jnp.where(keep, 1.0, 0.0).astype(BF16)

    s_parts = [_dot(qm, bufs[0][slot, j].astype(BF16)) for j in range(pages)]
    if mode == "fox":
        lf = jnp.concatenate([bufs[2][slot, j] for j in range(pages)], axis=0)
        cum = _dot_exact_rhs(lf, upper(0))
        carry = carry_s[...]
        for j in range(pages):
            cj = cum[j * SUBLANES:(j + 1) * SUBLANES]
            s_parts[j] = s_parts[j] - expand_rows((cj + carry) * LOG2E)
            carry = carry + cj[:, PAGE - 1:PAGE]
        carry_s[...] = carry
        s = jnp.concatenate(s_parts, axis=1)
    else:
        s = jnp.concatenate(s_parts, axis=1) + bias_ref[g]
    update(s, [functools.partial(lambda j: bufs[1][slot, j], j) for j in range(pages)])

    @pl.when(g == ng - 1)
    def _():
        sn = _dot(qm, knew_ref[...].astype(BF16))
        col = lax.broadcasted_iota(I32, (nrow, PAGE), 1)
        t_row = lax.broadcasted_iota(I32, (nrow, PAGE), 0) % SUBLANES
        rel = t_row - (col - s_idx * 4)
        ok = (rel >= 0) & (col >= s_idx * 4) & (t_row < 4)
        if mode == "fox":
            lfn = small_ref[...].T[SUBLANES:2 * SUBLANES, :]
            ckn = expand_rows((_dot_exact_rhs(lfn, upper(4)) + carry_s[...]) * LOG2E)
            cq = jnp.sum(jnp.where(rel == 0, ckn, 0.0), axis=-1, keepdims=True)
            m_s[...] = m_s[...] + cq
            sn = sn - ckn + cq
        else:
            bn = biasn_ref[...]
            add = jnp.zeros_like(sn)
            for r in range(4):
                add = jnp.where(rel == r, bn[:, r:r + 1], add)
            sn = sn + add
        sn = jnp.where(ok, sn, NEG_INF)
        update(sn, [lambda: vnew_ref[...]])

        o1 = acc_s[...] / l_s[...]
        if mode == "da":
            a = o1[0:4 * SUBLANES] - _lam(lamp_ref, lam_init) * o1[4 * SUBLANES:]
        else:
            a = o1
        lane = lax.broadcasted_iota(I32, (SUBLANES, ATT_W), 1)
        o8 = jnp.zeros((SUBLANES, ATT_W), F32)
        for h in range(N_HEADS):
            o8 = jnp.where((lane >= h * HEAD_W) & (lane < (h + 1) * HEAD_W),
                           a[h * SUBLANES:(h + 1) * SUBLANES], o8)
        y = _head_norm(o8, N_HEADS) * nw_ref[...]
        if mode == "da":
            y = y * (1.0 - lam_init)
        o_ref[...] = y[0:4] * _silu(gate_ref[...])


def _decode(mode, layer, page_table, qm, k_cache, v_cache, knew, vnew, gate, norm_w, extra,
            pages, lam_init=0.0):
    nseq, nrow, _ = qm.shape
    npages = page_table.shape[0] // nseq
    ng = npages // pages

    def const(shape):
        return pl.BlockSpec(shape, lambda t, pt: (0,) * len(shape))

    def first(shape):
        return pl.BlockSpec((None,) + shape, lambda t, pt: (0,) * (len(shape) + 1))

    def per_seq(rows):
        return pl.BlockSpec((None, rows, ATT_W), lambda t, pt: (t // ng, 0, 0))

    hbm = pl.BlockSpec(memory_space=pl.ANY)
    in_specs = [per_seq(nrow), hbm, hbm]
    args = [qm, k_cache, v_cache]
    bufs = [pltpu.VMEM((2, pages, ATT_W, PAGE), F32), pltpu.VMEM((2, pages, ATT_W, PAGE), F32)]
    state = [pltpu.VMEM((nrow, 1), F32), pltpu.VMEM((nrow, 1), F32), pltpu.VMEM((nrow, ATT_W), F32)]
    if mode == "fox":
        lf_cache, small = extra
        in_specs += [hbm, first((PAGE, LANES))]
        args += [lf_cache, small]
        bufs.append(pltpu.VMEM((2, pages, SUBLANES, PAGE), F32))
        state.append(pltpu.VMEM((SUBLANES, PAGE), F32))
    else:
        bias, bias_new, lamp = extra
        in_specs += [const(bias.shape), const(bias_new.shape), _layer_spec(lamp, layer, 1)]
        args += [bias, bias_new, lamp]
    in_specs += [first((ATT_W, PAGE)), first((ATT_W, PAGE)), per_seq(4), _layer_spec(norm_w, layer, 1)]
    args += [knew, vnew, gate, norm_w]
    scratch = bufs + [pltpu.SemaphoreType.DMA((len(bufs), 2))] + state
    return pl.pallas_call(
        functools.partial(_decode_kernel, mode=mode, layer=layer, pages=pages, ng=ng, lam_init=lam_init),
        grid_spec=pltpu.PrefetchScalarGridSpec(
            num_scalar_prefetch=1, grid=(nseq * ng,), in_specs=in_specs,
            out_specs=per_seq(4), scratch_shapes=scratch),
        out_shape=jax.ShapeDtypeStruct((nseq, 4, ATT_W), F32),
        compiler_params=_cp(("arbitrary",)),
        name="decode_" + mode,
    )(page_table, *args)


def _pad_lanes(v, width=LANES):
    return jnp.pad(v, ((0, 0), (0, width - v.shape[-1])))


def _masked_queries(q, ncomp):
    nseq = q.shape[0]
    width = HEAD_W // ncomp
    lane = jnp.arange(ATT_W)
    q8 = jnp.pad(q, ((0, 0), (0, SUBLANES - 4), (0, 0)))
    blocks = []
    for c in range(ncomp):
        for h in range(N_HEADS):
            lo = h * HEAD_W + c * width
            blocks.append(jnp.where((lane >= lo) & (lane < lo + width), q8, jnp.zeros_like(q8)))
    return jnp.concatenate(blocks, axis=1).reshape(nseq, ncomp * N_HEADS * SUBLANES, ATT_W)


def kernel(x_prompt, x_sample, c_prompt, c_sample, cache_da_k, cache_da_v, cache_fox_k, cache_fox_v,
           cache_fox_logf, state_ssm, state_conv, page_table, rel_bias, w_ada, b_ada, g_pre, g_post,
           w_in, w_out, da_lam_q1, da_lam_k1, da_lam_q2, da_lam_k2, da_subln, conv_w, conv_b, dt_bias,
           a_log, d_skip, ssm_norm, fox_fbias, fox_norm):
    depth = w_in.shape[0]
    bsz, seq, _ = x_prompt.shape
    nseq, dec_len, _ = x_sample.shape
    nphys = cache_da_k.shape[1]
    npages = page_table.shape[1]
    past = npages * PAGE
    assert dec_len == 4 and nseq * dec_len == PAGE
    tile = 512 if seq % 512 == 0 else 128
    chunk = 128
    chunk_s = 16
    pages = next((p for p in (64, 32) if npages % p == 0), npages)

    def cols(a, b):
        return w_in[:, :, a:b]

    w_rows = jnp.concatenate([cols(C_DAG, C_Z), cols(C_Z, C_XBC), cols(C_XBC, C_DT),
                              cols(C_FXG, C_FXF)], axis=-1).astype(BF16)
    w_kvt = jnp.swapaxes(jnp.concatenate([cols(C_DAQ, C_DAG), cols(C_FXQ, C_FXG)], axis=-1), 1, 2).astype(BF16)
    w_small = jnp.pad(jnp.concatenate([cols(C_DT, C_FXQ), cols(C_FXF, C_FXF + 4)], axis=-1),
                      ((0, 0), (0, 0), (0, LANES - 12))).astype(BF16)
    small_bias = _pad_lanes(jnp.concatenate([dt_bias, fox_fbias], axis=-1))
    w_out_b = w_out.astype(BF16)
    lamp = jnp.pad(jnp.stack([da_lam_q1, da_lam_k1, da_lam_q2, da_lam_k2], axis=1),
                   ((0, 0), (0, SUBLANES - 4), (0, LANES - DA_QK)))
    conv_w8 = jnp.pad(conv_w, ((0, 0), (0, SUBLANES - CONV_W), (0, 0)))
    a_log_p = _pad_lanes(a_log)
    d_skip_e = jnp.repeat(d_skip, HEAD_W, axis=-1)
    da_nw2 = jnp.tile(da_subln, (1, 2))
    da_nw4 = jnp.tile(da_subln, (1, 4))
    fx_nw2 = jnp.tile(fox_norm, (1, 2))
    fx_nw4 = jnp.tile(fox_norm, (1, 4))

    def pages_t(c):
        return jnp.transpose(c, (0, 1, 3, 4, 2)).reshape(depth, nphys, ATT_W, PAGE)

    kc_da, vc_da, kc_fx, vc_fx = (pages_t(c) for c in (cache_da_k, cache_da_v, cache_fox_k, cache_fox_v))
    lf_cache = jnp.pad(jnp.transpose(cache_fox_logf, (0, 1, 3, 2)),
                       ((0, 0), (0, 0), (0, SUBLANES - N_HEADS), (0, 0)))
    conv_init_s = jnp.pad(state_conv, ((0, 0), (0, 0), (SUBLANES - 3, 0), (0, 0)))
    ssm_init_s = state_ssm.reshape(depth, nseq, 4, LANES, SSM_STATE)
    conv_init_p = jnp.zeros((bsz, SUBLANES, CONV_CH), F32)
    ssm_init_p = jnp.zeros((bsz, 4, LANES, SSM_STATE), F32)

    nc = bsz + nseq
    nc_pad = -(-nc // SUBLANES) * SUBLANES
    c_all = jnp.pad(jnp.concatenate([c_prompt, c_sample], axis=0), ((0, nc_pad - nc), (0, 0)))
    mod = _adaln(c_all, w_ada, b_ada)

    ii = jnp.arange(tile, dtype=I32)
    d = ii[None, :] - ii[:, None]
    rel_tiles = jnp.concatenate([d + 2 * tile, d + tile, d], axis=0)
    bias_tiles = _t5_bias(rel_bias, rel_tiles, tile).reshape(N_HEADS, 3, tile, tile) * LOG2E
    tt = jnp.arange(SUBLANES, dtype=I32)
    rel_dec = jnp.where(tt[:, None] < dec_len, past + tt[:, None] - jnp.arange(past, dtype=I32)[None, :], 1)
    bias_dec = _t5_bias(rel_bias, rel_dec, SUBLANES)
    bias_dec = jnp.tile(bias_dec.reshape(N_HEADS * SUBLANES, past), (2, 1))
    bias_dec = jnp.transpose(bias_dec.reshape(2 * N_HEADS * SUBLANES, npages // pages, pages * PAGE),
                             (1, 0, 2)) * LOG2E
    bias_new = jnp.tile(jnp.repeat(_pad_lanes(rel_bias[:4].T), SUBLANES, axis=0), (2, 1)) * LOG2E
    fox_mask = jnp.stack([jnp.zeros((tile, tile), F32), jnp.zeros((tile, tile), F32),
                          jnp.where(d >= 0, 0.0, NEG_INF).astype(F32)])[None]

    mod_p = jnp.transpose(mod[:, :bsz].reshape(depth, bsz, 3, 1, D_MODEL), (0, 2, 1, 3, 4))
    mod_s = jnp.transpose(jnp.repeat(mod[:, bsz:nc], dec_len, axis=1).reshape(depth, 1, nseq * dec_len, 3, D_MODEL),
                          (0, 3, 1, 2, 4))
    proj_params = (g_pre[:, None], w_rows, w_kvt, w_small, small_bias[:, None])
    ssd_params = (conv_w8, conv_b[:, None], a_log_p[:, None], d_skip_e[:, None], ssm_norm[:, None])
    da_nw2, da_nw4, fx_nw2, fx_nw4 = (a[:, None] for a in (da_nw2, da_nw4, fx_nw2, fx_nw4))
    g_post3 = g_post[:, None]
    pt_flat = page_table.reshape(-1)

    xp = x_prompt
    xs = x_sample.reshape(1, nseq * dec_len, D_MODEL)
    stacks = tuple(jnp.zeros((depth, bsz, ATT_W, seq), F32) for _ in range(4))
    rest_p = [[] for _ in range(3)]
    outs_s = [[] for _ in range(7)]
    for l in range(depth):
        lam_init = 0.8 - 0.6 * math.exp(-0.3 * l)

        res = _in_proj(xp, mod_p, l, *proj_params, tile, stacked=stacks)
        kdar, gda, z, xbc, small, kfxr, gfx, qdat, _, _, vdab, qfxt, _, _, vfxb = res
        stacks = (res[8], res[9], res[12], res[13])
        ckrep, cqrow = _cumsum(small, tile)
        o_da = _flash("da", l, qdat, kdar, vdab, gda, da_nw2, bias_tiles, (lamp,), tile, lam_init)
        o_fx = _flash("fox", l, qfxt, kfxr, vfxb, gfx, fx_nw2, fox_mask, (ckrep, cqrow), tile)
        o_ssm, ssm_fin = _ssd(l, xbc, z, small, conv_init_p, ssm_init_p, *ssd_params, chunk, None)
        xp = _out_proj(l, o_da, o_ssm, o_fx, xp, mod_p, w_out_b, g_post3, tile)
        for lst, val in zip(rest_p, (small, ssm_fin, xbc[:, seq - 3:, :])):
            lst.append(val)

        (_, gda, z, xbc, small, _, gfx, qdat, kda, vda, _, qfxt, kfx, vfx, _) = _in_proj(
            xs, mod_s, l, *proj_params, PAGE)
        qm_da = _masked_queries(qdat[0].T.reshape(nseq, dec_len, ATT_W), 2)
        qm_fx = _masked_queries(qfxt[0].T.reshape(nseq, dec_len, ATT_W), 1)
        o_da = _decode("da", l, pt_flat, qm_da, kc_da, vc_da, kda, vda,
                       gda.reshape(nseq, dec_len, ATT_W), da_nw4, (bias_dec, bias_new, lamp), pages, lam_init)
        o_fx = _decode("fox", l, pt_flat, qm_fx, kc_fx, vc_fx, kfx, vfx,
                       gfx.reshape(nseq, dec_len, ATT_W), fx_nw4, (lf_cache, small), pages)

        def pad_seq(a):
            a = a.reshape(nseq, dec_len, a.shape[-1])
            return jnp.pad(a, ((0, 0), (0, chunk_s - dec_len), (0, 0)))

        o_ssm, ssm_fin = _ssd(l, pad_seq(xbc), pad_seq(z), pad_seq(small), conv_init_s, ssm_init_s,
                              *ssd_params, chunk_s, dec_len)
        o_ssm = o_ssm[:, :dec_len].reshape(1, nseq * dec_len, SSM_W)
        xs = _out_proj(l, o_da.reshape(1, nseq * dec_len, ATT_W), o_ssm, o_fx.reshape(1, nseq * dec_len, ATT_W),
                       xs, mod_s, w_out_b, g_post3, PAGE)
        for lst, val in zip(outs_s, (kda, vda, kfx, vfx, small, ssm_fin, xbc)):
            lst.append(val)

    def heads(a, n, s):
        return jnp.transpose(a.reshape(depth, n, N_HEADS, HEAD_W, s), (0, 1, 4, 2, 3))

    small_p, ssm_p, conv_p = (jnp.stack(v) for v in rest_p)
    out_p = tuple(heads(a, bsz, seq) for a in stacks) + (
        small_p[..., 8:12], ssm_p.reshape(depth, bsz, SSM_HEADS, HEAD_W, SSM_STATE), conv_p)
    kda_s, vda_s, kfx_s, vfx_s, small_s, ssm_s, xbc_s = (jnp.stack(v) for v in outs_s)
    ntok = nseq * dec_len
    out_s = tuple(heads(a, 1, ntok).reshape(depth, nseq, dec_len, N_HEADS, HEAD_W)
                  for a in (kda_s, vda_s, kfx_s, vfx_s)) + (
        small_s.reshape(depth, nseq, dec_len, LANES)[..., 8:12],
        ssm_s.reshape(depth, nseq, SSM_HEADS, HEAD_W, SSM_STATE),
        xbc_s.reshape(depth, nseq, dec_len, CONV_CH)[:, :, dec_len - 3:, :])
    return (xp, xs.reshape(nseq, dec_len, D_MODEL)) + out_p + out_s
```
